```python
import math
import jax
import jax.numpy as jnp
from jax import lax
import numpy as np

D_MODEL = 1024
BATCH = 4
SEQ = 4096
DEPTH = 2
DEC_BATCH = 128
DEC_SEQ = 8
PAST_LEN = 16384
PAGE_SIZE = 128

N_BRANCH = 4
BR_WIDTH = D_MODEL // 4
MOBA_HEADS = 4
MOBA_KV_HEADS = 2
MOBA_HD = BR_WIDTH // MOBA_HEADS
MOBA_GROUP = MOBA_HEADS // MOBA_KV_HEADS
MOBA_BLOCK = 256
MOBA_TOPK = 3
MLA_HEADS = 4
MLA_NOPE = 64
MLA_ROPE = 32
MLA_V = BR_WIDTH // MLA_HEADS
MLA_Q_LORA = D_MODEL // 4
MLA_KV_LORA = D_MODEL // 8
ROPE_THETA = 10000.0
CONV_DIM = BR_WIDTH
CONV_W = 3
GMLP_GROUPS = 4
GMLP_DIM = BR_WIDTH
GMLP_GD = GMLP_DIM // GMLP_GROUPS
CHUNK = 128
N_BUCKETS = 32
T5_MAX_DIST = 128
D_FF = 2816
PLE_DIM = 256
Q_BLOCK = 128
N_NORMS = 8
EPS = 1e-6
IN_WIDTH = (MOBA_HEADS * MOBA_HD + 2 * MOBA_KV_HEADS * MOBA_HD + MLA_Q_LORA + MLA_KV_LORA + MLA_ROPE
            + 3 * CONV_DIM + 2 * GMLP_DIM)

kernel_name = 'hybrid_moba_mla_conv_gmlp_decode_step'


def rmsnorm(x, g):
    xf = x.astype(jnp.float32)
    y = xf * lax.rsqrt(jnp.mean(xf * xf, axis=-1, keepdims=True) + EPS)
    return (y * g.astype(jnp.float32)).astype(x.dtype)


def swiglu(x, w_g, w_u, w_d):
    return (jax.nn.silu(x @ w_g) * (x @ w_u)) @ w_d


def t5_bucket(dist):
    n = jnp.maximum(dist, 0)
    exact = N_BUCKETS // 2
    nf = jnp.maximum(n, exact).astype(jnp.float32)
    large = exact + (jnp.log(nf / exact) / math.log(T5_MAX_DIST / exact) * (N_BUCKETS - exact)).astype(jnp.int32)
    return jnp.where(n < exact, n, jnp.minimum(large, N_BUCKETS - 1))


def rope(x, pos):
    half = x.shape[-1] // 2
    freq = ROPE_THETA ** (-jnp.arange(half, dtype=jnp.float32) / half)
    ang = pos.astype(jnp.float32)[:, None] * freq
    ang = ang.reshape((ang.shape[0],) + (1,) * (x.ndim - 3) + (half,))
    cos, sin = jnp.cos(ang), jnp.sin(ang)
    xf = x.astype(jnp.float32)
    x1, x2 = xf[..., :half], xf[..., half:]
    return jnp.concatenate([x1 * cos - x2 * sin, x2 * cos + x1 * sin], axis=-1).astype(x.dtype)


def gather_pages(cache, layer, page_table):
    rows = cache[layer, page_table]
    return rows.reshape((rows.shape[0], rows.shape[1] * rows.shape[2]) + rows.shape[3:])


def concat_rows(parts, multiple):
    n = sum(a.shape[1] for a in parts)
    extra = -n % multiple
    if extra:
        parts = parts + [jnp.zeros((parts[0].shape[0], extra) + parts[0].shape[2:], parts[0].dtype)]
    return parts[0] if len(parts) == 1 else jnp.concatenate(parts, axis=1)


def sweep_queries(fn, qs, q_pos):
    L = q_pos.shape[0]
    if L <= Q_BLOCK or L % Q_BLOCK:
        return fn(*qs, q_pos)
    n = L // Q_BLOCK
    blocks = tuple(jnp.moveaxis(a.reshape((a.shape[0], n, Q_BLOCK) + a.shape[2:]), 1, 0) for a in qs)
    out = lax.map(lambda args: fn(*args[0], args[1]), (blocks, q_pos.reshape(n, Q_BLOCK)))
    out = jnp.moveaxis(out, 0, 1)
    return out.reshape((out.shape[0], L) + out.shape[3:])


def moba_attend(q, q_pos, k_all, v_all, k_means, rel_bias):
    b, nq = q.shape[:2]
    nb = k_means.shape[1]
    qf = q.astype(jnp.float32)
    qg = qf.reshape(b, nq, MOBA_KV_HEADS, MOBA_GROUP, MOBA_HD)
    gate = jnp.einsum('bqkgd,bnkd->bqkgn', qg, k_means).reshape(b, nq, MOBA_HEADS, nb)
    q_blk = q_pos // MOBA_BLOCK
    fully_past = jnp.arange(nb)[None, :] < q_blk[:, None]
    gate = jnp.where(fully_past[None, :, None, :], gate, -jnp.inf)
    top_s, top_i = lax.top_k(gate, min(MOBA_TOPK, nb))
    sel_ok = jnp.isfinite(top_s)
    r = jnp.arange(MOBA_BLOCK)
    heads = jnp.arange(MOBA_HEADS)
    sel_pos = top_i[..., None] * MOBA_BLOCK + r
    b_idx = jnp.arange(b)[:, None, None, None, None]
    kv_idx = (heads // MOBA_GROUP)[None, None, :, None, None]
    k_sel = k_all[b_idx, sel_pos, kv_idx].astype(jnp.float32)
    v_sel = v_all[b_idx, sel_pos, kv_idx]
    own_pos = q_blk[:, None] * MOBA_BLOCK + r
    k_own = k_all[:, own_pos].astype(jnp.float32)
    v_own = v_all[:, own_pos]
    scale = MOBA_HD ** -0.5
    bias_sel = rel_bias[t5_bucket(q_pos[None, :, None, None, None] - sel_pos), heads[None, None, :, None, None]]
    s_sel = jnp.einsum('bqhd,bqhjrd->bqhjr', qf, k_sel) * scale + bias_sel.astype(jnp.float32)
    s_sel = jnp.where(sel_ok[..., None], s_sel, -jnp.inf)
    bias_own = jnp.transpose(rel_bias[t5_bucket(q_pos[:, None] - own_pos)], (0, 2, 1))
    s_own = (jnp.einsum('bqkgd,bqrkd->bqkgr', qg, k_own).reshape(b, nq, MOBA_HEADS, MOBA_BLOCK) * scale
             + bias_own.astype(jnp.float32))
    s_own = jnp.where((own_pos <= q_pos[:, None])[None, :, None, :], s_own, -jnp.inf)
    n_sel = s_sel.shape[3] * MOBA_BLOCK
    probs = jax.nn.softmax(jnp.concatenate([s_sel.reshape(b, nq, MOBA_HEADS, n_sel), s_own], axis=-1), axis=-1)
    p_sel = probs[..., :n_sel].reshape(s_sel.shape)
    p_own = probs[..., n_sel:].reshape(b, nq, MOBA_KV_HEADS, MOBA_GROUP, MOBA_BLOCK)
    o = (jnp.einsum('bqhjr,bqhjrd->bqhd', p_sel, v_sel)
         + jnp.einsum('bqkgr,bqrkd->bqkgd', p_own, v_own).reshape(b, nq, MOBA_HEADS, MOBA_HD))
    return o.astype(q.dtype)


def mla_attend(q_lat, q_rope, q_pos, c_kv, k_rope):
    k_pos = jnp.arange(c_kv.shape[1])
    s = (jnp.einsum('bqhc,bkc->bhqk', q_lat, c_kv, preferred_element_type=jnp.float32)
         + jnp.einsum('bqhr,bkr->bhqk', q_rope, k_rope, preferred_element_type=jnp.float32))
    s = s * (MLA_NOPE + MLA_ROPE) ** -0.5
    s = jnp.where((k_pos[None, :] <= q_pos[:, None])[None, None], s, -jnp.inf)
    p = jax.nn.softmax(s, axis=-1)
    return jnp.einsum('bhqk,bkc->bqhc', p, c_kv).astype(q_lat.dtype)


def token_mixers(h, pos, w_in, w_gate, mla_q_norm, mla_kv_norm, w_mla_q_b, w_mla_kv_b, conv_w,
                 gmlp_ws, gmlp_b, w_branch, w_out, rel_bias, past):
    b, L, _ = h.shape
    if past is None:
        ck = cv = cl = cr = sc = page_table = layer = None
    else:
        ck, cv, cl, cr, sc, page_table, layer = past

    def context(cache, new, multiple):
        parts = [new] if past is None else [gather_pages(cache, layer, page_table), new]
        return concat_rows(parts, multiple)

    widths = [MOBA_HEADS * MOBA_HD, MOBA_KV_HEADS * MOBA_HD, MOBA_KV_HEADS * MOBA_HD, MLA_Q_LORA,
              MLA_KV_LORA, MLA_ROPE, CONV_DIM, CONV_DIM, CONV_DIM, GMLP_DIM, GMLP_DIM]
    cuts = np.cumsum(widths)[:-1].tolist()
    mq, mk, mv, mqa, mkv, mkr, cb, cc, cx, gu, gv = jnp.split(h @ w_in, cuts, axis=-1)

    q = mq.reshape(b, L, MOBA_HEADS, MOBA_HD)
    k_new = mk.reshape(b, L, MOBA_KV_HEADS, MOBA_HD)
    v_new = mv.reshape(b, L, MOBA_KV_HEADS, MOBA_HD)
    k_ctx = context(ck, k_new, MOBA_BLOCK)
    v_ctx = context(cv, v_new, MOBA_BLOCK)
    nb = k_ctx.shape[1] // MOBA_BLOCK
    k_means = k_ctx.reshape(k_ctx.shape[0], nb, MOBA_BLOCK, MOBA_KV_HEADS, MOBA_HD).astype(jnp.float32).mean(axis=2)
    o_moba = sweep_queries(lambda qq, pp: moba_attend(qq, pp, k_ctx, v_ctx, k_means, rel_bias), (q,), pos)
    o_moba = o_moba.reshape(b, L, MOBA_HEADS * MOBA_HD)

    cq = rmsnorm(mqa, mla_q_norm)
    qh = (cq @ w_mla_q_b).reshape(b, L, MLA_HEADS, MLA_NOPE + MLA_ROPE)
    q_nope = qh[..., :MLA_NOPE]
    q_r = rope(qh[..., MLA_NOPE:], pos)
    c_new = rmsnorm(mkv, mla_kv_norm)
    kr_new = rope(mkr, pos)
    w_uk = w_mla_kv_b[..., :MLA_NOPE]
    w_uv = w_mla_kv_b[..., MLA_NOPE:]
    q_lat = jnp.einsum('blhn,chn->blhc', q_nope, w_uk)
    c_ctx = context(cl, c_new, 1)
    kr_ctx = context(cr, kr_new, 1)
    o_lat = sweep_queries(lambda a, rr, pp: mla_attend(a, rr, pp, c_ctx, kr_ctx), (q_lat, q_r), pos)
    o_mla = jnp.einsum('blhc,chv->blhv', o_lat, w_uv).reshape(b, L, MLA_HEADS * MLA_V)

    hc = cc * cx
    prev = jnp.zeros((b, CONV_W - 1, CONV_DIM), hc.dtype) if past is None else sc[layer].astype(hc.dtype)
    h_ext = jnp.concatenate([prev, hc], axis=1)
    conv = conv_w[0] * h_ext[:, 0:L]
    for j in range(1, CONV_W):
        conv = conv + conv_w[j] * h_ext[:, j:j + L]
    o_conv = cb * conv
    conv_state = h_ext[:, -(CONV_W - 1):]

    u = jax.nn.gelu(gu)
    vg = jax.nn.gelu(gv)
    vc = concat_rows([vg], CHUNK)
    nc = vc.shape[1] // CHUNK
    vc = vc.reshape(b, nc, CHUNK, GMLP_GROUPS, GMLP_GD)
    ws = gmlp_ws * jnp.tril(jnp.ones((CHUNK, CHUNK), gmlp_ws.dtype))
    mixed = jnp.einsum('gts,bcsgd->bctgd', ws, vc) + jnp.transpose(gmlp_b)[:, :, None]
    o_gmlp = u * mixed.reshape(b, nc * CHUNK, GMLP_DIM)[:, :L]

    branches = jnp.stack([o_moba, o_mla, o_conv, o_gmlp], axis=2)
    gates = jax.nn.sigmoid((h @ w_gate).reshape(b, L, N_BRANCH, D_MODEL))
    merged = jnp.einsum('blnw,nwd->blnd', branches, w_branch)
    out = jnp.sum(gates * merged, axis=2) @ w_out
    return out, (k_new, v_new, c_new, kr_new, conv_state, vg)


def decoder_layer(x, p, pos, lw, rel_bias, past):
    (ng, wfg, wfu, wfd, w_in, w_gate, qn, kvn, wqb, wkvb, cw, ws, bs, wbr, wo, wpg, wpp) = lw
    x = x + 0.5 * rmsnorm(swiglu(rmsnorm(x, ng[0]), wfg[0], wfu[0], wfd[0]), ng[1])
    mix, state = token_mixers(rmsnorm(x, ng[2]), pos, w_in, w_gate, qn, kvn, wqb, wkvb, cw, ws, bs,
                              wbr, wo, rel_bias, past)
    x = x + rmsnorm(mix, ng[3])
    x = x + 0.5 * rmsnorm(swiglu(rmsnorm(x, ng[4]), wfg[1], wfu[1], wfd[1]), ng[5])
    hp = rmsnorm(x, ng[6])
    x = x + rmsnorm(jax.nn.sigmoid(hp @ wpg) * (p @ wpp), ng[7])
    return x, state


def setup_inputs(seed: int = 0) -> dict:
    key = jax.random.key(seed)
    ks = jax.random.split(key, 28)
    f32 = jnp.float32
    n_pages = PAST_LEN // PAGE_SIZE
    n_used = DEC_BATCH * n_pages
    n_pool = n_used + max(1, n_used // 4)

    def nrm(k, shape, scale):
        return jax.random.normal(k, shape, f32) * scale

    def gain(k, shape):
        return 1.0 + 0.05 * jax.random.normal(k, shape, f32)

    page_table = jax.random.permutation(ks[7], n_pool)[:n_used].reshape(DEC_BATCH, n_pages).astype(jnp.int32)
    return {
        'x_prompt': jax.random.normal(ks[0], (BATCH, SEQ, D_MODEL), f32),
        'x_sample': jax.random.normal(ks[1], (DEC_BATCH, DEC_SEQ, D_MODEL), f32),
        'cache_moba_k': jax.random.normal(ks[2], (DEPTH, n_pool, PAGE_SIZE, MOBA_KV_HEADS, MOBA_HD), f32),
        'cache_moba_v': jax.random.normal(ks[3], (DEPTH, n_pool, PAGE_SIZE, MOBA_KV_HEADS, MOBA_HD), f32),
        'cache_mla_latent': jax.random.normal(ks[4], (DEPTH, n_pool, PAGE_SIZE, MLA_KV_LORA), f32),
        'cache_mla_rope': jax.random.normal(ks[5], (DEPTH, n_pool, PAGE_SIZE, MLA_ROPE), f32),
        'state_conv': jax.random.normal(ks[6], (DEPTH, DEC_BATCH, CONV_W - 1, CONV_DIM), f32),
        'page_table': page_table,
        'p_prompt': jax.random.normal(ks[8], (DEPTH, BATCH, SEQ, PLE_DIM), f32),
        'p_sample': jax.random.normal(ks[9], (DEPTH, DEC_BATCH, DEC_SEQ, PLE_DIM), f32),
        'rel_bias': nrm(ks[10], (N_BUCKETS, MOBA_HEADS), 0.5),
        'norm_g': gain(ks[11], (DEPTH, N_NORMS, D_MODEL)),
        'w_ffn_gate': nrm(ks[12], (DEPTH, 2, D_MODEL, D_FF), D_MODEL ** -0.5),
        'w_ffn_up': nrm(ks[13], (DEPTH, 2, D_MODEL, D_FF), D_MODEL ** -0.5),
        'w_ffn_down': nrm(ks[14], (DEPTH, 2, D_FF, D_MODEL), D_FF ** -0.5),
        'w_in': nrm(ks[15], (DEPTH, D_MODEL, IN_WIDTH), D_MODEL ** -0.5),
        'w_gate': nrm(ks[16], (DEPTH, D_MODEL, N_BRANCH * D_MODEL), D_MODEL ** -0.5),
        'mla_q_norm': gain(ks[17], (DEPTH, MLA_Q_LORA)),
        'mla_kv_norm': gain(ks[18], (DEPTH, MLA_KV_LORA)),
        'w_mla_q_b': nrm(ks[19], (DEPTH, MLA_Q_LORA, MLA_HEADS * (MLA_NOPE + MLA_ROPE)), MLA_Q_LORA ** -0.5),
        'w_mla_kv_b': nrm(ks[20], (DEPTH, MLA_KV_LORA, MLA_HEADS, MLA_NOPE + MLA_V), MLA_KV_LORA ** -0.5),
        'conv_w': nrm(ks[21], (DEPTH, CONV_W, CONV_DIM), CONV_W ** -0.5),
        'gmlp_ws': nrm(ks[22], (DEPTH, GMLP_GROUPS, CHUNK, CHUNK), CHUNK ** -0.5),
        'gmlp_b': 1.0 + 0.1 * jax.random.normal(ks[23], (DEPTH, GMLP_GROUPS, CHUNK), f32),
        'w_branch': nrm(ks[24], (DEPTH, N_BRANCH, BR_WIDTH, D_MODEL), BR_WIDTH ** -0.5),
        'w_out': nrm(ks[25], (DEPTH, D_MODEL, D_MODEL), D_MODEL ** -0.5),
        'w_ple_gate': nrm(ks[26], (DEPTH, D_MODEL, D_MODEL), D_MODEL ** -0.5),
        'w_ple_proj': nrm(ks[27], (DEPTH, PLE_DIM, D_MODEL), PLE_DIM ** -0.5),
    }


def reference(x_prompt, x_sample, cache_moba_k, cache_moba_v, cache_mla_latent, cache_mla_rope, state_conv,
              page_table, p_prompt, p_sample, rel_bias, norm_g, w_ffn_gate, w_ffn_up, w_ffn_down, w_in, w_gate,
              mla_q_norm, mla_kv_norm, w_mla_q_b, w_mla_kv_b, conv_w, gmlp_ws, gmlp_b, w_branch, w_out,
              w_ple_gate, w_ple_proj):
    past_len = page_table.shape[1] * cache_moba_k.shape[2]
    pos_prompt = jnp.arange(x_prompt.shape[1], dtype=jnp.int32)
    pos_sample = past_len + jnp.arange(x_sample.shape[1], dtype=jnp.int32)
    yp, ys = x_prompt, x_sample
    st_p, st_s = [], []
    for i in range(DEPTH):
        lw = (norm_g[i], w_ffn_gate[i], w_ffn_up[i], w_ffn_down[i], w_in[i], w_gate[i], mla_q_norm[i],
              mla_kv_norm[i], w_mla_q_b[i], w_mla_kv_b[i], conv_w[i], gmlp_ws[i], gmlp_b[i], w_branch[i],
              w_out[i], w_ple_gate[i], w_ple_proj[i])
        yp, sp = decoder_layer(yp, p_prompt[i], pos_prompt, lw, rel_bias, None)
        ys, ss = decoder_layer(ys, p_sample[i], pos_sample, lw, rel_bias,
                               (cache_moba_k, cache_moba_v, cache_mla_latent, cache_mla_rope, state_conv,
                                page_table, i))
        st_p.append(sp)
        st_s.append(ss)

    def stack(states, j):
        return jnp.stack([s[j] for s in states], axis=0)

    return (yp, ys,
            stack(st_p, 0), stack(st_p, 1), stack(st_p, 2), stack(st_p, 3), stack(st_p, 4),
            stack(st_s, 0), stack(st_s, 1), stack(st_s, 2), stack(st_s, 3), stack(st_s, 4), stack(st_s, 5))
```

```python
import functools
import math

import jax
import jax.numpy as jnp
from jax import lax
from jax.experimental import pallas as pl
from jax.experimental.pallas import tpu as pltpu

F32 = jnp.float32
BF16 = jnp.bfloat16
EPS = 1e-6
NEG_INF = float("-inf")

N_BRANCH = 4
MOBA_HEADS = 4
MOBA_HD = 64
MOBA_BLOCK = 256
MOBA_TOPK = 3
MLA_HEADS = 4
MLA_NOPE = 64
MLA_ROPE = 32
MLA_LAT = 128
ROPE_THETA = 10000.0
GMLP_GROUPS = 4
GMLP_CHUNK = 128
N_BUCKETS = 32
T5_MAX_DIST = 128
T5_EXACT = N_BUCKETS // 2

LANES = 128
SUBLANES = 8
VMEM_LIMIT_BYTES = 56 * 1024 * 1024

ROW_TILE = 512
ATT_TILE = 256
DEC_KEY_CHUNK = 4096

C_MQ, C_MK, C_MV, C_MQA, C_MKV, C_CB, C_CC, C_CX, C_GU, C_GV, C_MKR, C_END = (
    0, 256, 384, 512, 768, 896, 1152, 1408, 1664, 1920, 2176, 2304)

NT_DIMS = (((1,), (1,)), ((), ()))


def _rms(x, g):
    return x * lax.rsqrt(jnp.mean(x * x, axis=-1, keepdims=True) + EPS) * g


def _dot(a, b):
    return jnp.dot(a, b, preferred_element_type=F32)


def _dot_nt(a, b, precision=None):
    return lax.dot_general(a, b, NT_DIMS, precision=precision, preferred_element_type=F32)


def _iota(shape, dim):
    return lax.broadcasted_iota(jnp.int32, shape, dim)


def _t5_bias(dist, rb_of_bucket):
    n = jnp.maximum(dist, 0)
    nf = jnp.maximum(n, T5_EXACT).astype(F32)
    large = T5_EXACT + (jnp.log(nf / T5_EXACT) / math.log(T5_MAX_DIST / T5_EXACT)
                        * (N_BUCKETS - T5_EXACT)).astype(jnp.int32)
    bucket = jnp.where(n < T5_EXACT, n, jnp.minimum(large, N_BUCKETS - 1))
    out = jnp.zeros(dist.shape, F32)
    for k in range(N_BUCKETS):
        out = jnp.where(bucket == k, rb_of_bucket(k), out)
    return out


def _head_rows_to_kv_lanes(q):
    qa, qb = q[:, :LANES], q[:, LANES:]
    lo = _iota(qa.shape, 1) < MOBA_HD
    zero = jnp.zeros_like(qa)
    return [jnp.where(lo, qa, zero), jnp.where(lo, pltpu.roll(qa, MOBA_HD, 1), zero),
            jnp.where(lo, zero, pltpu.roll(qb, MOBA_HD, 1)), jnp.where(lo, zero, qb)]


def _kv_lanes_to_head_cols(o):
    lo = _iota(o[0].shape, 1) < MOBA_HD
    return (jnp.where(lo, o[0], pltpu.roll(o[1], MOBA_HD, 1)),
            jnp.where(lo, pltpu.roll(o[2], MOBA_HD, 1), o[3]))


def _top3_columns(work, col, ncol):
    picks = []
    colf = col.astype(F32)
    for _ in range(MOBA_TOPK):
        mx = jnp.max(work, axis=-1, keepdims=True)
        ok = mx > NEG_INF
        idx = jnp.min(jnp.where((work == mx) & ok, colf, float(ncol)), axis=-1, keepdims=True).astype(jnp.int32)
        work = jnp.where(col == idx, NEG_INF, work)
        picks.append((idx, ok))
    return picks


def _ffn_kernel(x_ref, gpre_ref, gpost_ref, wg_ref, wu_ref, wd_ref, o_ref):
    x = x_ref[...]
    xn = _rms(x, gpre_ref[...]).astype(BF16)
    g = _dot(xn, wg_ref[...])
    u = _dot(xn, wu_ref[...])
    a = (jax.nn.silu(g) * u).astype(BF16)
    y = _dot(a, wd_ref[...])
    o_ref[...] = x + 0.5 * _rms(y, gpost_ref[...])


def _ple_kernel(x_ref, p_ref, gpre_ref, gpost_ref, wpg_ref, wpp_ref, o_ref):
    x = x_ref[...]
    hp = _rms(x, gpre_ref[...]).astype(BF16)
    gate = jax.nn.sigmoid(_dot(hp, wpg_ref[...]))
    pp = _dot(p_ref[...].astype(BF16), wpp_ref[...])
    o_ref[...] = x + _rms(gate * pp, gpost_ref[...])


def _merge_kernel(x_ref, om_ref, ol_ref, oc_ref, og_ref, gpre_ref, gpost_ref,
                  wgate_ref, wuv_ref, wbr_ref, wout_ref, o_ref):
    x = x_ref[...]
    d = x.shape[1]
    h = _rms(x, gpre_ref[...]).astype(BF16)
    o_mla = _dot(ol_ref[...].astype(BF16), wuv_ref[...])
    branches = (om_ref[...], o_mla, oc_ref[...], og_ref[...])
    acc = None
    for n in range(N_BRANCH):
        gate = jax.nn.sigmoid(_dot(h, wgate_ref[:, n * d:(n + 1) * d]))
        term = gate * _dot(branches[n].astype(BF16), wbr_ref[n])
        acc = term if acc is None else acc + term
    out = _dot(acc.astype(BF16), wout_ref[...])
    o_ref[...] = x + _rms(out, gpost_ref[...])


def _mixer_pre_kernel(x_ref, g_ref, win_ref, qn_ref, kvn_ref, wq_ref, wuk_ref, cos_ref, sin_ref,
                      cw_ref, st1_ref, st2_ref, wst_ref, gb_ref,
                      q_o, k_o, v_o, kb_o, vb_o, c_o, kr_o, qcat_o, kcat_o, oc_o, og_o, vg_o, hc_o,
                      carry_ref, *, tm, short_seq, tiles_per_seq, ch):
    x = x_ref[...]
    h = _rms(x, g_ref[...]).astype(BF16)
    proj = _dot(h, win_ref[...])

    mk = proj[:, C_MK:C_MV]
    mv = proj[:, C_MV:C_MQA]
    q_o[...] = proj[:, C_MQ:C_MK]
    k_o[...] = mk
    v_o[...] = mv
    kb_o[...] = mk.astype(BF16)
    vb_o[...] = mv.astype(BF16)

    cos = cos_ref[...]
    sin = sin_ref[...]
    lane = _iota((tm, LANES), 1)
    half = MLA_ROPE // 2

    def rope(v):
        swapped = jnp.where(lane < half, pltpu.roll(v, LANES - half, 1), pltpu.roll(v, half, 1))
        return v * cos + swapped * sin

    cq = _rms(proj[:, C_MQA:C_MKV], qn_ref[...]).astype(BF16)
    qh = _dot(cq, wq_ref[...])
    nope_w = MLA_HEADS * MLA_NOPE
    qlat = _dot(qh[:, :nope_w].astype(BF16), wuk_ref[...])
    scale = (MLA_NOPE + MLA_ROPE) ** -0.5
    for hh in range(MLA_HEADS):
        qcat_o[:, 2 * LANES * hh:2 * LANES * hh + LANES] = (qlat[:, LANES * hh:LANES * (hh + 1)] * scale).astype(BF16)
        qr = rope(qh[:, nope_w + LANES * hh:nope_w + LANES * (hh + 1)])
        qcat_o[:, 2 * LANES * hh + LANES:2 * LANES * (hh + 1)] = (qr * scale).astype(BF16)
    c_new = _rms(proj[:, C_MKV:C_CB], kvn_ref[...])
    kr = rope(proj[:, C_MKR:C_END])
    c_o[...] = c_new
    kr_o[...] = kr
    kcat_o[:, :LANES] = c_new.astype(BF16)
    kcat_o[:, LANES:] = kr.astype(BF16)

    hc = proj[:, C_CC:C_CX] * proj[:, C_CX:C_GU]
    r1 = pltpu.roll(hc, 1, 0)
    r2 = pltpu.roll(hc, 2, 0)
    row = _iota(hc.shape, 0)
    if short_seq:
        rs = row % SUBLANES
        s1 = jnp.where(rs < 1, st1_ref[...], r1)
        s2 = jnp.where(rs < 2, st2_ref[...], r2)
    else:
        @pl.when(pl.program_id(0) % tiles_per_seq == 0)
        def _():
            carry_ref[...] = jnp.zeros(carry_ref.shape, F32)
        prev = carry_ref[...]
        s1 = jnp.where(row < 1, prev[SUBLANES - 1:SUBLANES, :], r1)
        s2 = jnp.where(row == 0, prev[SUBLANES - 2:SUBLANES - 1, :],
                       jnp.where(row == 1, prev[SUBLANES - 1:SUBLANES, :], r2))
        carry_ref[...] = hc[tm - SUBLANES:, :]
    cw = cw_ref[...]
    conv = cw[0:1, :] * s2 + cw[1:2, :] * s1 + cw[2:3, :] * hc
    oc_o[...] = proj[:, C_CB:C_CC] * conv
    hc_o[...] = hc

    u = jax.nn.gelu(proj[:, C_GU:C_GV])
    vg = jax.nn.gelu(proj[:, C_GV:C_MKR])
    vg_o[...] = vg
    wst = wst_ref[...]
    gb = gb_ref[...]
    group = _iota((ch, gb.shape[1]), 1) // (gb.shape[1] // GMLP_GROUPS)
    for c in range(tm // ch):
        mixed_all = _dot(wst, vg[c * ch:(c + 1) * ch, :].astype(BF16))
        mixed = gb
        for g in range(GMLP_GROUPS):
            mixed = mixed + jnp.where(group == g, mixed_all[g * ch:(g + 1) * ch, :], 0.0)
        og_o[c * ch:(c + 1) * ch, :] = u[c * ch:(c + 1) * ch, :] * mixed


def _moba_prompt_kernel(rb_ref, q_ref, k_ref, kb_ref, vb_ref, o_ref, km_ref, bown_ref, bprev_ref, *, nb):
    b = pl.program_id(0)
    i = pl.program_id(1)
    bl = MOBA_BLOCK

    @pl.when((b == 0) & (i == 0))
    def _():
        dist = _iota((bl, bl), 0) - _iota((bl, bl), 1)
        for h in range(MOBA_HEADS):
            bown_ref[h] = _t5_bias(dist, lambda k, h=h: rb_ref[k, h])
            bprev_ref[h] = _t5_bias(dist + bl, lambda k, h=h: rb_ref[k, h])

    @pl.when(i == 0)
    def _():
        km_ref[...] = jnp.zeros(km_ref.shape, F32)
        for n in range(nb):
            km_ref[n:n + 1, :] = jnp.sum(k_ref[n * bl:(n + 1) * bl, :], axis=0, keepdims=True) * (1.0 / bl)

    qm = _head_rows_to_kv_lanes(q_ref[...])
    col = _iota((bl, LANES), 1)
    rr = _iota((bl, bl), 0)
    cc = _iota((bl, bl), 1)
    k_own = kb_ref[pl.ds(pl.multiple_of(i * bl, bl), bl), :]
    v_own = vb_ref[pl.ds(pl.multiple_of(i * bl, bl), bl), :]
    outs = []
    for h in range(MOBA_HEADS):
        gate = _dot_nt(qm[h], km_ref[...], precision=lax.Precision.HIGHEST)
        picks = _top3_columns(jnp.where(col < i, gate, NEG_INF), col, LANES)
        sel = jnp.zeros((bl, LANES), F32)
        for idx, _ in picks:
            sel = jnp.where(col == idx, 1.0, sel)
        qs = (qm[h] * (MOBA_HD ** -0.5)).astype(BF16)

        s = _dot_nt(qs, k_own) + bown_ref[h]
        s = jnp.where(cc <= rr, s, NEG_INF)
        m = jnp.max(s, axis=-1, keepdims=True)
        p = jnp.exp(s - m)
        l = jnp.sum(p, axis=-1, keepdims=True)
        acc = _dot(p.astype(BF16), v_own)

        def body(n, carry, h=h, qs=qs, sel=sel):
            m, l, acc = carry
            start = pl.multiple_of(n * bl, bl)
            kn = kb_ref[pl.ds(start, bl), :]
            vn = vb_ref[pl.ds(start, bl), :]
            s = _dot_nt(qs, kn) + jnp.where(n == i - 1, bprev_ref[h], rb_ref[N_BUCKETS - 1, h])
            chosen = jnp.sum(jnp.where(col == n, sel, 0.0), axis=-1, keepdims=True) > 0.5
            s = jnp.where(chosen, s, NEG_INF)
            m_new = jnp.maximum(m, jnp.max(s, axis=-1, keepdims=True))
            alpha = jnp.exp(m - m_new)
            p = jnp.exp(s - m_new)
            return (m_new, alpha * l + jnp.sum(p, axis=-1, keepdims=True),
                    alpha * acc + _dot(p.astype(BF16), vn))

        m, l, acc = lax.fori_loop(0, i, body, (m, l, acc))
        outs.append(acc / l)
    oa, ob = _kv_lanes_to_head_cols(outs)
    o_ref[:, :LANES] = oa
    o_ref[:, LANES:] = ob


def _mla_prompt_kernel(q_ref, k_ref, o_ref, *, tq):
    i = pl.program_id(1)
    rr = _iota((tq, tq), 0)
    cc = _iota((tq, tq), 1)
    k_own = k_ref[pl.ds(pl.multiple_of(i * tq, tq), tq), :]
    for h in range(MLA_HEADS):
        q = q_ref[:, 2 * LANES * h:2 * LANES * (h + 1)]
        s = jnp.where(cc <= rr, _dot_nt(q, k_own), NEG_INF)
        m = jnp.max(s, axis=-1, keepdims=True)
        p = jnp.exp(s - m)
        l = jnp.sum(p, axis=-1, keepdims=True)
        acc = _dot(p.astype(BF16), k_own[:, :MLA_LAT])

        def body(j, carry, q=q):
            m, l, acc = carry
            kj = k_ref[pl.ds(pl.multiple_of(j * tq, tq), tq), :]
            s = _dot_nt(q, kj)
            m_new = jnp.maximum(m, jnp.max(s, axis=-1, keepdims=True))
            alpha = jnp.exp(m - m_new)
            p = jnp.exp(s - m_new)
            return (m_new, alpha * l + jnp.sum(p, axis=-1, keepdims=True),
                    alpha * acc + _dot(p.astype(BF16), kj[:, :MLA_LAT]))

        m, l, acc = lax.fori_loop(0, i, body, (m, l, acc))
        o_ref[:, MLA_LAT * h:MLA_LAT * (h + 1)] = acc / l


def _page_copies(pt_ref, caches, bufs, sem_ref, b, slot, page_base, n_pages, page, do_start):
    def one(p, carry):
        src = page_base + pt_ref[b, p]
        for a, (cache_ref, buf_ref) in enumerate(zip(caches, bufs)):
            cp = pltpu.make_async_copy(cache_ref.at[src], buf_ref.at[slot, pl.ds(p * page, page)],
                                       sem_ref.at[slot, a])
            if do_start:
                cp.start()
            else:
                cp.wait()
        return carry
    lax.fori_loop(0, n_pages, one, 0)


def _prefetch_schedule(pt_ref, caches, bufs, sem_ref, page_base, n_pages, page):
    b = pl.program_id(0)
    nb = pl.num_programs(0)
    slot = b % 2
    args = (pt_ref, caches, bufs, sem_ref)

    @pl.when(b == 0)
    def _():
        _page_copies(*args, b, slot, page_base, n_pages, page, True)

    @pl.when(b + 1 < nb)
    def _():
        _page_copies(*args, b + 1, 1 - slot, page_base, n_pages, page, True)

    _page_copies(*args, b, slot, page_base, n_pages, page, False)
    return slot


def _moba_decode_kernel(pt_ref, rb_ref, q_ref, kn_ref, vn_ref, kc_hbm, vc_hbm, o_ref,
                        kbuf, vbuf, sem, km_ref, s_ref, *, page_base, n_pages, page, ck, ls):
    slot = _prefetch_schedule(pt_ref, (kc_hbm, vc_hbm), (kbuf, vbuf), sem, page_base, n_pages, page)
    bl = MOBA_BLOCK
    nk = n_pages * page
    nbc = nk // bl
    rows = MOBA_HEADS * ls

    km_ref[...] = jnp.zeros(km_ref.shape, F32)
    for n in range(nbc):
        km_ref[n:n + 1, :] = jnp.sum(kbuf[slot, n * bl:(n + 1) * bl, :], axis=0, keepdims=True) * (1.0 / bl)

    qstack = jnp.concatenate(_head_rows_to_kv_lanes(q_ref[...]), axis=0)
    nkm = km_ref.shape[0]
    gate = _dot_nt(qstack, km_ref[...], precision=lax.Precision.HIGHEST)
    colg = _iota((rows, nkm), 1)
    picks = _top3_columns(jnp.where(colg < nbc, gate, NEG_INF), colg, nkm)

    rid = _iota((rows, 1), 0)
    head = rid // ls
    qi = rid % ls

    def rb_col(k):
        out = jnp.zeros((rows, 1), F32)
        for h in range(MOBA_HEADS):
            out = jnp.where(head == h, rb_ref[k, h], out)
        return out

    far_bias = rb_col(N_BUCKETS - 1)
    qs = (qstack * (MOBA_HD ** -0.5)).astype(BF16)
    m = jnp.full((rows, 1), NEG_INF, F32)
    n_chunks = nk // ck
    for c in range(n_chunks):
        s = _dot_nt(qs, kbuf[slot, c * ck:(c + 1) * ck, :].astype(BF16))
        blk = (_iota((rows, ck), 1) + c * ck) // bl
        chosen = (blk == picks[0][0]) & picks[0][1]
        for idx, ok in picks[1:]:
            chosen = chosen | ((blk == idx) & ok)
        s = jnp.where(chosen, s + far_bias, NEG_INF)
        s_ref[:, c * ck:(c + 1) * ck] = s
        m = jnp.maximum(m, jnp.max(s, axis=-1, keepdims=True))
    last = nk - bl
    dist_last = (bl + qi) - _iota((rows, bl), 1)
    s_last = s_ref[:, last:] + (_t5_bias(dist_last, rb_col) - far_bias)
    s_ref[:, last:] = s_last
    m = jnp.maximum(m, jnp.max(s_last, axis=-1, keepdims=True))

    k_new = kn_ref[...]
    v_new = vn_ref[...]
    colo = _iota((rows, ls), 1)
    q32 = qstack * (MOBA_HD ** -0.5)
    s_own = jnp.zeros((rows, ls), F32)
    for j in range(ls):
        sj = jnp.sum(q32 * k_new[j:j + 1, :], axis=-1, keepdims=True)
        s_own = jnp.where(colo == j, sj, s_own)
    s_own = s_own + _t5_bias(qi - colo, rb_col)
    s_own = jnp.where(colo <= qi, s_own, NEG_INF)
    m = jnp.maximum(m, jnp.max(s_own, axis=-1, keepdims=True))

    p_own = jnp.exp(s_own - m)
    l = jnp.sum(p_own, axis=-1, keepdims=True)
    acc = jnp.zeros((rows, LANES), F32)
    for j in range(ls):
        acc = acc + p_own[:, j:j + 1] * v_new[j:j + 1, :]
    for c in range(n_chunks):
        p = jnp.exp(s_ref[:, c * ck:(c + 1) * ck] - m)
        l = l + jnp.sum(p, axis=-1, keepdims=True)
        acc = acc + _dot(p.astype(BF16), vbuf[slot, c * ck:(c + 1) * ck, :].astype(BF16))
    o = acc / l
    oa, ob = _kv_lanes_to_head_cols([o[h * ls:(h + 1) * ls, :] for h in range(MOBA_HEADS)])
    o_ref[:, :LANES] = oa
    o_ref[:, LANES:] = ob


def _mla_decode_kernel(pt_ref, q_ref, cn_ref, krn_ref, cc_hbm, rc_hbm, o_ref,
                       cbuf, rbuf, sem, s_ref, *, page_base, n_pages, page, ck, ls):
    slot = _prefetch_schedule(pt_ref, (cc_hbm, rc_hbm), (cbuf, rbuf), sem, page_base, n_pages, page)
    nk = n_pages * page
    rows = MLA_HEADS * ls
    q = q_ref[...]
    qlat = jnp.concatenate([q[:, 2 * LANES * h:2 * LANES * h + LANES] for h in range(MLA_HEADS)], axis=0)
    qrp = jnp.concatenate([q[:, 2 * LANES * h + LANES:2 * LANES * (h + 1)] for h in range(MLA_HEADS)], axis=0)
    qr = qrp[:, :MLA_ROPE]
    qi = _iota((rows, 1), 0) % ls

    m = jnp.full((rows, 1), NEG_INF, F32)
    n_chunks = nk // ck
    for c in range(n_chunks):
        s = (_dot_nt(qlat, cbuf[slot, c * ck:(c + 1) * ck, :].astype(BF16))
             + _dot_nt(qr, rbuf[slot, c * ck:(c + 1) * ck, :].astype(BF16)))
        s_ref[:, c * ck:(c + 1) * ck] = s
        m = jnp.maximum(m, jnp.max(s, axis=-1, keepdims=True))

    c_new = cn_ref[...]
    kr_new = krn_ref[...]
    colo = _iota((rows, ls), 1)
    qlat32 = qlat.astype(F32)
    qrp32 = qrp.astype(F32)
    s_own = jnp.zeros((rows, ls), F32)
    for j in range(ls):
        sj = (jnp.sum(qlat32 * c_new[j:j + 1, :], axis=-1, keepdims=True)
              + jnp.sum(qrp32 * kr_new[j:j + 1, :], axis=-1, keepdims=True))
        s_own = jnp.where(colo == j, sj, s_own)
    s_own = jnp.where(colo <= qi, s_own, NEG_INF)
    m = jnp.maximum(m, jnp.max(s_own, axis=-1, keepdims=True))

    p_own = jnp.exp(s_own - m)
    l = jnp.sum(p_own, axis=-1, keepdims=True)
    acc = jnp.zeros((rows, MLA_LAT), F32)
    for j in range(ls):
        acc = acc + p_own[:, j:j + 1] * c_new[j:j + 1, :]
    for c in range(n_chunks):
        p = jnp.exp(s_ref[:, c * ck:(c + 1) * ck] - m)
        l = l + jnp.sum(p, axis=-1, keepdims=True)
        acc = acc + _dot(p.astype(BF16), cbuf[slot, c * ck:(c + 1) * ck, :].astype(BF16))
    o = acc / l
    for h in range(MLA_HEADS):
        o_ref[:, MLA_LAT * h:MLA_LAT * (h + 1)] = o[h * ls:(h + 1) * ls, :]


def _params(*sem):
    return pltpu.CompilerParams(dimension_semantics=sem, vmem_limit_bytes=VMEM_LIMIT_BYTES)


def _row_spec(tm, width):
    return pl.BlockSpec((tm, width), lambda i: (i, 0))


def _const_spec(shape):
    nd = len(shape)
    return pl.BlockSpec(shape, lambda i: (0,) * nd, pipeline_mode=pl.Buffered(1))


def _row_tile(t):
    tm = min(ROW_TILE, t)
    assert t % tm == 0 and tm % SUBLANES == 0, (t, tm)
    return tm


def _ffn_step(x, gpre, gpost, wg, wu, wd):
    t, d = x.shape
    tm = _row_tile(t)
    return pl.pallas_call(
        _ffn_kernel, grid=(t // tm,),
        in_specs=[_row_spec(tm, d), _const_spec(gpre.shape), _const_spec(gpost.shape),
                  _const_spec(wg.shape), _const_spec(wu.shape), _const_spec(wd.shape)],
        out_specs=_row_spec(tm, d), out_shape=jax.ShapeDtypeStruct((t, d), F32),
        compiler_params=_params("arbitrary"), name="ffn_step",
    )(x, gpre, gpost, wg, wu, wd)


def _ple_step(x, p, gpre, gpost, wpg, wpp):
    t, d = x.shape
    tm = _row_tile(t)
    return pl.pallas_call(
        _ple_kernel, grid=(t // tm,),
        in_specs=[_row_spec(tm, d), _row_spec(tm, p.shape[1]), _const_spec(gpre.shape), _const_spec(gpost.shape),
                  _const_spec(wpg.shape), _const_spec(wpp.shape)],
        out_specs=_row_spec(tm, d), out_shape=jax.ShapeDtypeStruct((t, d), F32),
        compiler_params=_params("arbitrary"), name="ple_step",
    )(x, p, gpre, gpost, wpg, wpp)


def _merge_step(x, o_moba, o_lat, o_conv, o_gmlp, gpre, gpost, wgate, wuv, wbr, wout):
    t, d = x.shape
    tm = _row_tile(t)
    return pl.pallas_call(
        _merge_kernel, grid=(t // tm,),
        in_specs=[_row_spec(tm, d), _row_spec(tm, o_moba.shape[1]), _row_spec(tm, o_lat.shape[1]),
                  _row_spec(tm, o_conv.shape[1]), _row_spec(tm, o_gmlp.shape[1]),
                  _const_spec(gpre.shape), _const_spec(gpost.shape), _const_spec(wgate.shape),
                  _const_spec(wuv.shape), _const_spec(wbr.shape), _const_spec(wout.shape)],
        out_specs=_row_spec(tm, d), out_shape=jax.ShapeDtypeStruct((t, d), F32),
        compiler_params=_params("arbitrary"), name="merge_step",
    )(x, o_moba, o_lat, o_conv, o_gmlp, gpre, gpost, wgate, wuv, wbr, wout)


def _mixer_pre(x, g, lw, cos_tab, sin_tab, st1, st2, wst, gbias, seq_len):
    t, d = x.shape
    tm = _row_tile(t)
    short_seq = seq_len < tm
    if short_seq:
        assert seq_len == SUBLANES and st1.shape == (t, 256), (seq_len, st1.shape)
        tiles_per_seq = 1
        st_spec = _row_spec(tm, 256)
        tab_spec = _const_spec(cos_tab.shape)
    else:
        assert seq_len % tm == 0
        tiles_per_seq = seq_len // tm
        st_spec = _const_spec(st1.shape)
        tab_spec = pl.BlockSpec((tm, LANES), lambda i: (i % tiles_per_seq, 0))
    ch = wst.shape[1]
    assert tm % ch == 0
    widths = [(256, F32), (128, F32), (128, F32), (128, BF16), (128, BF16), (128, F32), (128, F32),
              (1024, BF16), (256, BF16), (256, F32), (256, F32), (256, F32), (256, F32)]
    kern = functools.partial(_mixer_pre_kernel, tm=tm, short_seq=short_seq, tiles_per_seq=tiles_per_seq, ch=ch)
    return pl.pallas_call(
        kern, grid=(t // tm,),
        in_specs=[_row_spec(tm, d), _const_spec(g.shape), _const_spec(lw["w_in"].shape),
                  _const_spec(lw["qn"].shape), _const_spec(lw["kvn"].shape), _const_spec(lw["wq"].shape),
                  _const_spec(lw["wuk"].shape), tab_spec, tab_spec, _const_spec(lw["conv_w"].shape),
                  st_spec, st_spec, _const_spec(wst.shape), _const_spec(gbias.shape)],
        out_specs=[_row_spec(tm, w) for w, _ in widths],
        out_shape=[jax.ShapeDtypeStruct((t, w), dt) for w, dt in widths],
        scratch_shapes=[pltpu.VMEM((SUBLANES, 256), F32)],
        compiler_params=_params("arbitrary"), name="mixer_pre",
    )(x, g, lw["w_in"], lw["qn"], lw["kvn"], lw["wq"], lw["wuk"], cos_tab, sin_tab, lw["conv_w"],
      st1, st2, wst, gbias)


def _moba_prompt(rel_bias, q, k, kb, vb, batch, seq):
    bl = MOBA_BLOCK
    assert seq % bl == 0 and seq // bl <= LANES
    nb = seq // bl
    return pl.pallas_call(
        functools.partial(_moba_prompt_kernel, nb=nb), grid=(batch, nb),
        in_specs=[pl.BlockSpec(memory_space=pltpu.SMEM),
                  pl.BlockSpec((bl, 256), lambda b, i: (b * nb + i, 0)),
                  pl.BlockSpec((seq, 128), lambda b, i: (b, 0)),
                  pl.BlockSpec((seq, 128), lambda b, i: (b, 0)),
                  pl.BlockSpec((seq, 128), lambda b, i: (b, 0))],
        out_specs=pl.BlockSpec((bl, 256), lambda b, i: (b * nb + i, 0)),
        out_shape=jax.ShapeDtypeStruct((batch * seq, 256), F32),
        scratch_shapes=[pltpu.VMEM((LANES, LANES), F32),
                        pltpu.VMEM((MOBA_HEADS, bl, bl), F32), pltpu.VMEM((MOBA_HEADS, bl, bl), F32)],
        compiler_params=_params("arbitrary", "arbitrary"), name="moba_prompt",
    )(rel_bias, q, k, kb, vb)


def _mla_prompt(qcat, kcat, batch, seq):
    tq = min(ATT_TILE, seq)
    assert seq % tq == 0
    nq = seq // tq
    return pl.pallas_call(
        functools.partial(_mla_prompt_kernel, tq=tq), grid=(batch, nq),
        in_specs=[pl.BlockSpec((tq, qcat.shape[1]), lambda b, i: (b * nq + i, 0)),
                  pl.BlockSpec((seq, kcat.shape[1]), lambda b, i: (b, 0))],
        out_specs=pl.BlockSpec((tq, MLA_HEADS * MLA_LAT), lambda b, i: (b * nq + i, 0)),
        out_shape=jax.ShapeDtypeStruct((batch * seq, MLA_HEADS * MLA_LAT), F32),
        compiler_params=_params("arbitrary", "arbitrary"), name="mla_prompt",
    )(qcat, kcat)


def _decode_geometry(page_table, page, ls):
    bd, n_pages = page_table.shape
    nk = n_pages * page
    assert nk % MOBA_BLOCK == 0 and ls == SUBLANES, (nk, ls)
    ck = min(DEC_KEY_CHUNK, nk)
    assert nk % ck == 0
    return bd, n_pages, nk, ck


def _moba_decode(page_table, rel_bias, q, k_new, v_new, cache_k, cache_v, layer, ls):
    n_pool, page = cache_k.shape[1], cache_k.shape[2]
    bd, n_pages, nk, ck = _decode_geometry(page_table, page, ls)
    kc = cache_k.reshape(-1, page, 128)
    vc = cache_v.reshape(-1, page, 128)
    nkm = -(-(nk // MOBA_BLOCK) // LANES) * LANES
    rows = MOBA_HEADS * ls
    kern = functools.partial(_moba_decode_kernel, page_base=layer * n_pool, n_pages=n_pages, page=page, ck=ck, ls=ls)
    grid_spec = pltpu.PrefetchScalarGridSpec(
        num_scalar_prefetch=1, grid=(bd,),
        in_specs=[pl.BlockSpec(memory_space=pltpu.SMEM),
                  pl.BlockSpec((ls, 256), lambda b, pt: (b, 0)),
                  pl.BlockSpec((ls, 128), lambda b, pt: (b, 0)),
                  pl.BlockSpec((ls, 128), lambda b, pt: (b, 0)),
                  pl.BlockSpec(memory_space=pl.ANY), pl.BlockSpec(memory_space=pl.ANY)],
        out_specs=pl.BlockSpec((ls, 256), lambda b, pt: (b, 0)),
        scratch_shapes=[pltpu.VMEM((2, nk, 128), F32), pltpu.VMEM((2, nk, 128), F32),
                        pltpu.SemaphoreType.DMA((2, 2)),
                        pltpu.VMEM((nkm, 128), F32), pltpu.VMEM((rows, nk), F32)])
    return pl.pallas_call(
        kern, grid_spec=grid_spec, out_shape=jax.ShapeDtypeStruct((bd * ls, 256), F32),
        compiler_params=_params("arbitrary"), name="moba_decode",
    )(page_table, rel_bias, q, k_new, v_new, kc, vc)


def _mla_decode(page_table, qcat, c_new, kr_new, cache_lat, cache_rope, layer, ls):
    n_pool, page = cache_lat.shape[1], cache_lat.shape[2]
    bd, n_pages, nk, ck = _decode_geometry(page_table, page, ls)
    cc = cache_lat.reshape(-1, page, MLA_LAT)
    rc = cache_rope.reshape(-1, page, MLA_ROPE)
    rows = MLA_HEADS * ls
    kern = functools.partial(_mla_decode_kernel, page_base=layer * n_pool, n_pages=n_pages, page=page, ck=ck, ls=ls)
    grid_spec = pltpu.PrefetchScalarGridSpec(
        num_scalar_prefetch=1, grid=(bd,),
        in_specs=[pl.BlockSpec((ls, qcat.shape[1]), lambda b, pt: (b, 0)),
                  pl.BlockSpec((ls, 128), lambda b, pt: (b, 0)),
                  pl.BlockSpec((ls, 128), lambda b, pt: (b, 0)),
                  pl.BlockSpec(memory_space=pl.ANY), pl.BlockSpec(memory_space=pl.ANY)],
        out_specs=pl.BlockSpec((ls, MLA_HEADS * MLA_LAT), lambda b, pt: (b, 0)),
        scratch_shapes=[pltpu.VMEM((2, nk, MLA_LAT), F32), pltpu.VMEM((2, nk, MLA_ROPE), F32),
                        pltpu.SemaphoreType.DMA((2, 2)), pltpu.VMEM((rows, nk), F32)])
    return pl.pallas_call(
        kern, grid_spec=grid_spec, out_shape=jax.ShapeDtypeStruct((bd * ls, MLA_HEADS * MLA_LAT), F32),
        compiler_params=_params("arbitrary"), name="mla_decode",
    )(page_table, qcat, c_new, kr_new, cc, rc)


def _prep_layer(i, norm_g, w_ffn_gate, w_ffn_up, w_ffn_down, w_in, w_gate, mla_q_norm, mla_kv_norm,
                w_mla_q_b, w_mla_kv_b, conv_w, w_branch, w_out, w_ple_gate, w_ple_proj):
    d = w_in.shape[1]
    wi = w_in[i]
    mkr_lo, mkr_hi = 896, 928
    w_in_p = jnp.concatenate([wi[:, :mkr_lo], wi[:, mkr_hi:], wi[:, mkr_lo:mkr_hi],
                              jnp.zeros((d, LANES - MLA_ROPE), F32)], axis=1)
    assert w_in_p.shape[1] == C_END
    lat_q = w_mla_q_b.shape[1]
    wq4 = w_mla_q_b[i].reshape(lat_q, MLA_HEADS, MLA_NOPE + MLA_ROPE)
    wq_rope = jnp.pad(wq4[:, :, MLA_NOPE:], ((0, 0), (0, 0), (0, LANES - MLA_ROPE)))
    wq = jnp.concatenate([wq4[:, :, :MLA_NOPE].reshape(lat_q, -1), wq_rope.reshape(lat_q, -1)], axis=1)
    eye = jnp.eye(MLA_HEADS, dtype=F32)
    wkv = w_mla_kv_b[i]
    wuk = jnp.einsum("chn,hg->hngc", wkv[..., :MLA_NOPE], eye).reshape(MLA_HEADS * MLA_NOPE, MLA_HEADS * MLA_LAT)
    wuv = jnp.einsum("chv,hg->hcgv", wkv[..., MLA_NOPE:], eye).reshape(MLA_HEADS * MLA_LAT, -1)
    return {
        "g": norm_g[i][:, None, :],
        "wfg": w_ffn_gate[i].astype(BF16), "wfu": w_ffn_up[i].astype(BF16), "wfd": w_ffn_down[i].astype(BF16),
        "w_in": w_in_p.astype(BF16), "w_gate": w_gate[i].astype(BF16),
        "qn": mla_q_norm[i][None, :], "kvn": mla_kv_norm[i][None, :],
        "wq": wq.astype(BF16), "wuk": wuk.astype(BF16), "wuv": wuv.astype(BF16),
        "conv_w": conv_w[i], "w_branch": w_branch[i].astype(BF16), "w_out": w_out[i].astype(BF16),
        "wpg": w_ple_gate[i].astype(BF16), "wpp": w_ple_proj[i].astype(BF16),
    }


def _rope_tables(pos):
    half = MLA_ROPE // 2
    freq = ROPE_THETA ** (-jnp.arange(half, dtype=F32) / half)
    ang = pos.astype(F32)[:, None] * freq
    cos, sin = jnp.cos(ang), jnp.sin(ang)
    pad = jnp.zeros((pos.shape[0], LANES - MLA_ROPE), F32)
    return jnp.concatenate([cos, cos, pad], axis=1), jnp.concatenate([-sin, sin, pad], axis=1)


def _gmlp_tables(ws, bs, seq_len, ch):
    n = min(seq_len, ch)
    w = (ws * jnp.tril(jnp.ones((GMLP_CHUNK, GMLP_CHUNK), ws.dtype)))[:, :n, :n]
    b = jnp.repeat(jnp.transpose(bs)[:n], 256 // GMLP_GROUPS, axis=1)
    reps = ch // n
    if reps > 1:
        w = jnp.einsum("gts,ab->gatbs", w, jnp.eye(reps, dtype=ws.dtype)).reshape(GMLP_GROUPS, ch, ch)
        b = jnp.tile(b, (reps, 1))
    return w.reshape(GMLP_GROUPS * ch, ch).astype(BF16), b


def _layer(x, p, lw, rel_bias, tabs, attend):
    g = lw["g"]
    x = _ffn_step(x, g[0], g[1], lw["wfg"][0], lw["wfu"][0], lw["wfd"][0])
    (q, k, v, kb, vb, c_new, kr, qcat, kcat, o_conv, o_gmlp, vg, hc) = _mixer_pre(
        x, g[2], lw, tabs["cos"], tabs["sin"], tabs["st1"], tabs["st2"], tabs["wst"], tabs["gbias"], tabs["seq_len"])
    o_moba, o_lat = attend(q, k, v, kb, vb, c_new, kr, qcat, kcat)
    x = _merge_step(x, o_moba, o_lat, o_conv, o_gmlp, g[2], g[3], lw["w_gate"], lw["wuv"], lw["w_branch"], lw["w_out"])
    x = _ffn_step(x, g[4], g[5], lw["wfg"][1], lw["wfu"][1], lw["wfd"][1])
    x = _ple_step(x, p, g[6], g[7], lw["wpg"], lw["wpp"])
    return x, (k, v, c_new, kr[:, :MLA_ROPE], hc, vg)


def kernel(x_prompt, x_sample, cache_moba_k, cache_moba_v, cache_mla_latent, cache_mla_rope, state_conv, page_table, p_prompt, p_sample, rel_bias, norm_g, w_ffn_gate, w_ffn_up, w_ffn_down, w_in, w_gate, mla_q_norm, mla_kv_norm, w_mla_q_b, w_mla_kv_b, conv_w, gmlp_ws, gmlp_b, w_branch, w_out, w_ple_gate, w_ple_proj):
    depth = norm_g.shape[0]
    bp, lp, d = x_prompt.shape
    bd, ls, _ = x_sample.shape
    page = cache_moba_k.shape[2]
    past_len = page_table.shape[1] * page
    tp, ts = bp * lp, bd * ls

    cos_p, sin_p = _rope_tables(jnp.arange(lp, dtype=jnp.int32))
    cos_s, sin_s = _rope_tables(past_len + jnp.arange(ls, dtype=jnp.int32))
    tm_s = _row_tile(ts)
    cos_s, sin_s = jnp.tile(cos_s, (tm_s // ls, 1)), jnp.tile(sin_s, (tm_s // ls, 1))
    ch_p = min(GMLP_CHUNK, _row_tile(tp))
    ch_s = min(GMLP_CHUNK, tm_s)
    no_state = jnp.zeros((SUBLANES, 256), F32)

    yp = x_prompt.reshape(tp, d)
    ys = x_sample.reshape(ts, d)
    st_p, st_s = [], []
    for i in range(depth):
        lw = _prep_layer(i, norm_g, w_ffn_gate, w_ffn_up, w_ffn_down, w_in, w_gate, mla_q_norm, mla_kv_norm,
                         w_mla_q_b, w_mla_kv_b, conv_w, w_branch, w_out, w_ple_gate, w_ple_proj)
        wst_p, gb_p = _gmlp_tables(gmlp_ws[i], gmlp_b[i], lp, ch_p)
        wst_s, gb_s = _gmlp_tables(gmlp_ws[i], gmlp_b[i], ls, ch_s)
        prev = state_conv[i]
        st1 = jnp.pad(prev[:, 1:2], ((0, 0), (0, ls - 1), (0, 0))).reshape(ts, 256)
        st2 = jnp.pad(prev, ((0, 0), (0, ls - 2), (0, 0))).reshape(ts, 256)
        tabs_p = dict(cos=cos_p, sin=sin_p, st1=no_state, st2=no_state, wst=wst_p, gbias=gb_p, seq_len=lp)
        tabs_s = dict(cos=cos_s, sin=sin_s, st1=st1, st2=st2, wst=wst_s, gbias=gb_s, seq_len=ls)

        def attend_prompt(q, k, v, kb, vb, c_new, kr, qcat, kcat):
            return _moba_prompt(rel_bias, q, k, kb, vb, bp, lp), _mla_prompt(qcat, kcat, bp, lp)

        def attend_sample(q, k, v, kb, vb, c_new, kr, qcat, kcat, i=i):
            return (_moba_decode(page_table, rel_bias, q, k, v, cache_moba_k, cache_moba_v, i, ls),
                    _mla_decode(page_table, qcat, c_new, kr, cache_mla_latent, cache_mla_rope, i, ls))

        yp, sp = _layer(yp, p_prompt[i].reshape(tp, -1), lw, rel_bias, tabs_p, attend_prompt)
        ys, ss = _layer(ys, p_sample[i].reshape(ts, -1), lw, rel_bias, tabs_s, attend_sample)
        st_p.append(sp)
        st_s.append(ss)

    def stack(states, j, shape):
        return jnp.stack([s[j].reshape(shape) for s in states], axis=0)

    kv_p, kv_s = (bp, lp, 2, MOBA_HD), (bd, ls, 2, MOBA_HD)
    conv_p = jnp.stack([s[4].reshape(bp, lp, 256)[:, -2:] for s in st_p], axis=0)
    conv_s = jnp.stack([s[4].reshape(bd, ls, 256)[:, -2:] for s in st_s], axis=0)
    return (yp.reshape(bp, lp, d), ys.reshape(bd, ls, d),
            stack(st_p, 0, kv_p), stack(st_p, 1, kv_p), stack(st_p, 2, (bp, lp, MLA_LAT)),
            stack(st_p, 3, (bp, lp, MLA_ROPE)), conv_p,
            stack(st_s, 0, kv_s), stack(st_s, 1, kv_s), stack(st_s, 2, (bd, ls, MLA_LAT)),
            stack(st_s, 3, (bd, ls, MLA_ROPE)), conv_s, stack(st_s, 5, (bd, ls, 256)))
```

```python
import functools
import math

import jax
import jax.numpy as jnp
from jax import lax
from jax.experimental import pallas as pl
from jax.experimental.pallas import tpu as pltpu

F32 = jnp.float32
BF16 = jnp.bfloat16
EPS = 1e-6
NEG_INF = float("-inf")
MASKED = -1e30

N_BRANCH = 4
MOBA_HEADS = 4
MOBA_HD = 64
MOBA_BLOCK = 256
MOBA_TOPK = 3
MLA_HEADS = 4
MLA_NOPE = 64
MLA_ROPE = 32
MLA_LAT = 128
ROPE_THETA = 10000.0
GMLP_GROUPS = 4
GMLP_CHUNK = 128
N_BUCKETS = 32
T5_MAX_DIST = 128
T5_EXACT = N_BUCKETS // 2

LANES = 128
SUBLANES = 8
VMEM_LIMIT_BYTES = 56 * 1024 * 1024

ROW_TILE = 512
ATT_TILE = 256
DEC_KEY_CHUNK = 4096

C_MQ, C_MK, C_MV, C_MQA, C_MKV, C_CB, C_CC, C_CX, C_GU, C_GV, C_MKR, C_END = (
    0, 256, 384, 512, 768, 896, 1152, 1408, 1664, 1920, 2176, 2304)

NT_DIMS = (((1,), (1,)), ((), ()))


def _rms(x, g):
    return x * lax.rsqrt(jnp.mean(x * x, axis=-1, keepdims=True) + EPS) * g


def _dot(a, b):
    return jnp.dot(a, b, preferred_element_type=F32)


def _dot_nt(a, b, precision=None):
    return lax.dot_general(a, b, NT_DIMS, precision=precision, preferred_element_type=F32)


def _iota(shape, dim):
    return lax.broadcasted_iota(jnp.int32, shape, dim)


def _t5_bias(dist, rb_of_bucket):
    n = jnp.maximum(dist, 0)
    nf = jnp.maximum(n, T5_EXACT).astype(F32)
    large = T5_EXACT + (jnp.log(nf / T5_EXACT) / math.log(T5_MAX_DIST / T5_EXACT)
                        * (N_BUCKETS - T5_EXACT)).astype(jnp.int32)
    bucket = jnp.where(n < T5_EXACT, n, jnp.minimum(large, N_BUCKETS - 1))
    out = jnp.zeros(dist.shape, F32)
    for k in range(N_BUCKETS):
        out = jnp.where(bucket == k, rb_of_bucket(k), out)
    return out


def _head_rows_to_kv_lanes(q):
    qa, qb = q[:, :LANES], q[:, LANES:]
    lo = _iota(qa.shape, 1) < MOBA_HD
    zero = jnp.zeros_like(qa)
    return [jnp.where(lo, qa, zero), jnp.where(lo, pltpu.roll(qa, MOBA_HD, 1), zero),
            jnp.where(lo, zero, pltpu.roll(qb, MOBA_HD, 1)), jnp.where(lo, zero, qb)]


def _kv_lanes_to_head_cols(o):
    lo = _iota(o[0].shape, 1) < MOBA_HD
    return (jnp.where(lo, o[0], pltpu.roll(o[1], MOBA_HD, 1)),
            jnp.where(lo, pltpu.roll(o[2], MOBA_HD, 1), o[3]))


def _top3_columns(work, col, ncol):
    picks = []
    colf = col.astype(F32)
    for _ in range(MOBA_TOPK):
        mx = jnp.max(work, axis=-1, keepdims=True)
        ok = mx > NEG_INF
        idx = jnp.min(jnp.where((work == mx) & ok, colf, float(ncol)), axis=-1, keepdims=True).astype(jnp.int32)
        work = jnp.where(col == idx, NEG_INF, work)
        picks.append((idx, ok))
    return picks


def _ffn_kernel(x_ref, gpre_ref, gpost_ref, wg_ref, wu_ref, wd_ref, o_ref):
    x = x_ref[...]
    xn = _rms(x, gpre_ref[...]).astype(BF16)
    g = _dot(xn, wg_ref[...])
    u = _dot(xn, wu_ref[...])
    a = (jax.nn.silu(g) * u).astype(BF16)
    y = _dot(a, wd_ref[...])
    o_ref[...] = x + 0.5 * _rms(y, gpost_ref[...])


def _ple_kernel(x_ref, p_ref, gpre_ref, gpost_ref, wpg_ref, wpp_ref, o_ref):
    x = x_ref[...]
    hp = _rms(x, gpre_ref[...]).astype(BF16)
    gate = jax.nn.sigmoid(_dot(hp, wpg_ref[...]))
    pp = _dot(p_ref[...].astype(BF16), wpp_ref[...])
    o_ref[...] = x + _rms(gate * pp, gpost_ref[...])


def _merge_kernel(x_ref, om_ref, ol_ref, oc_ref, og_ref, gpre_ref, gpost_ref,
                  wgate_ref, wuv_ref, wbr_ref, wout_ref, o_ref):
    x = x_ref[...]
    d = x.shape[1]
    h = _rms(x, gpre_ref[...]).astype(BF16)
    o_mla = _dot(ol_ref[...].astype(BF16), wuv_ref[...])
    branches = (om_ref[...], o_mla, oc_ref[...], og_ref[...])
    acc = None
    for n in range(N_BRANCH):
        gate = jax.nn.sigmoid(_dot(h, wgate_ref[:, n * d:(n + 1) * d]))
        term = gate * _dot(branches[n].astype(BF16), wbr_ref[n])
        acc = term if acc is None else acc + term
    out = _dot(acc.astype(BF16), wout_ref[...])
    o_ref[...] = x + _rms(out, gpost_ref[...])


def _mixer_pre_kernel(x_ref, g_ref, win_ref, qn_ref, kvn_ref, wq_ref, wuk_ref, cos_ref, sin_ref,
                      cw_ref, st1_ref, st2_ref, wst_ref, gb_ref,
                      q_o, k_o, v_o, kb_o, vb_o, c_o, kr_o, qcat_o, kcat_o, oc_o, og_o, vg_o, hc_o,
                      carry_ref, *, tm, short_seq, tiles_per_seq, ch):
    x = x_ref[...]
    h = _rms(x, g_ref[...]).astype(BF16)
    proj = _dot(h, win_ref[...])

    mk = proj[:, C_MK:C_MV]
    mv = proj[:, C_MV:C_MQA]
    q_o[...] = proj[:, C_MQ:C_MK]
    k_o[...] = mk
    v_o[...] = mv
    kb_o[:, :LANES] = mk.astype(BF16)
    if short_seq:
        kb_o[:, LANES:] = jnp.zeros((tm, LANES), BF16)
    else:
        pos = (pl.program_id(0) % tiles_per_seq) * tm + _iota((tm, LANES), 0)
        kb_o[:, LANES:] = jnp.where(pos // MOBA_BLOCK == _iota((tm, LANES), 1), 1.0, 0.0).astype(BF16)
    vb_o[...] = mv.astype(BF16)

    cos = cos_ref[...]
    sin = sin_ref[...]
    lane = _iota((tm, LANES), 1)
    half = MLA_ROPE // 2

    def rope(v):
        swapped = jnp.where(lane < half, pltpu.roll(v, LANES - half, 1), pltpu.roll(v, half, 1))
        return v * cos + swapped * sin

    cq = _rms(proj[:, C_MQA:C_MKV], qn_ref[...]).astype(BF16)
    qh = _dot(cq, wq_ref[...])
    nope_w = MLA_HEADS * MLA_NOPE
    qlat = _dot(qh[:, :nope_w].astype(BF16), wuk_ref[...])
    scale = (MLA_NOPE + MLA_ROPE) ** -0.5
    for hh in range(MLA_HEADS):
        qcat_o[:, 2 * LANES * hh:2 * LANES * hh + LANES] = (qlat[:, LANES * hh:LANES * (hh + 1)] * scale).astype(BF16)
        qr = rope(qh[:, nope_w + LANES * hh:nope_w + LANES * (hh + 1)])
        qcat_o[:, 2 * LANES * hh + LANES:2 * LANES * (hh + 1)] = (qr * scale).astype(BF16)
    c_new = _rms(proj[:, C_MKV:C_CB], kvn_ref[...])
    kr = rope(proj[:, C_MKR:C_END])
    c_o[...] = c_new
    kr_o[...] = kr
    kcat_o[:, :LANES] = c_new.astype(BF16)
    kcat_o[:, LANES:] = kr.astype(BF16)

    hc = proj[:, C_CC:C_CX] * proj[:, C_CX:C_GU]
    r1 = pltpu.roll(hc, 1, 0)
    r2 = pltpu.roll(hc, 2, 0)
    row = _iota(hc.shape, 0)
    if short_seq:
        rs = row % SUBLANES
        s1 = jnp.where(rs < 1, st1_ref[...], r1)
        s2 = jnp.where(rs < 2, st2_ref[...], r2)
    else:
        @pl.when(pl.program_id(0) % tiles_per_seq == 0)
        def _():
            carry_ref[...] = jnp.zeros(carry_ref.shape, F32)
        prev = carry_ref[...]
        s1 = jnp.where(row < 1, prev[SUBLANES - 1:SUBLANES, :], r1)
        s2 = jnp.where(row == 0, prev[SUBLANES - 2:SUBLANES - 1, :],
                       jnp.where(row == 1, prev[SUBLANES - 1:SUBLANES, :], r2))
        carry_ref[...] = hc[tm - SUBLANES:, :]
    cw = cw_ref[...]
    conv = cw[0:1, :] * s2 + cw[1:2, :] * s1 + cw[2:3, :] * hc
    oc_o[...] = proj[:, C_CB:C_CC] * conv
    hc_o[...] = hc

    u = jax.nn.gelu(proj[:, C_GU:C_GV])
    vg = jax.nn.gelu(proj[:, C_GV:C_MKR])
    vg_o[...] = vg
    wst = wst_ref[...]
    gb = gb_ref[...]
    group = _iota((ch, gb.shape[1]), 1) // (gb.shape[1] // GMLP_GROUPS)
    for c in range(tm // ch):
        mixed_all = _dot(wst, vg[c * ch:(c + 1) * ch, :].astype(BF16))
        mixed = gb
        for g in range(GMLP_GROUPS):
            mixed = mixed + jnp.where(group == g, mixed_all[g * ch:(g + 1) * ch, :], 0.0)
        og_o[c * ch:(c + 1) * ch, :] = u[c * ch:(c + 1) * ch, :] * mixed


def _attend_first(s, v):
    m = jnp.max(s, axis=-1, keepdims=True)
    p = jnp.exp(s - m)
    return m, jnp.sum(p, axis=-1, keepdims=True), _dot(p.astype(BF16), v)


def _attend_more(carry, s, v):
    m, l, acc = carry
    m_new = jnp.maximum(m, jnp.max(s, axis=-1, keepdims=True))
    alpha = jnp.exp(m - m_new)
    p = jnp.exp(s - m_new)
    return m_new, alpha * l + jnp.sum(p, axis=-1, keepdims=True), alpha * acc + _dot(p.astype(BF16), v)


def _moba_prompt_kernel(rb_ref, q_ref, k_ref, kb_ref, vb_ref, o_ref, km_ref, bown_ref, bprev_ref, *, nb):
    b = pl.program_id(0)
    i = pl.program_id(1)
    bl = MOBA_BLOCK
    rows = MOBA_HEADS * bl

    @pl.when((b == 0) & (i == 0))
    def _():
        dist = _iota((bl, bl), 0) - _iota((bl, bl), 1)
        for h in range(MOBA_HEADS):
            far = rb_ref[N_BUCKETS - 1, h]
            bown_ref[h * bl:(h + 1) * bl, :] = _t5_bias(dist, lambda k, h=h: rb_ref[k, h]) - far
            bprev_ref[h * bl:(h + 1) * bl, :] = _t5_bias(dist + bl, lambda k, h=h: rb_ref[k, h]) - far

    @pl.when(i == 0)
    def _():
        km_ref[...] = jnp.zeros(km_ref.shape, F32)
        for n in range(nb):
            km_ref[n:n + 1, :] = jnp.sum(k_ref[n * bl:(n + 1) * bl, :], axis=0, keepdims=True) * (1.0 / bl)

    qstack = jnp.concatenate(_head_rows_to_kv_lanes(q_ref[...]), axis=0)
    col = _iota((rows, LANES), 1)
    gate = _dot_nt(qstack, km_ref[...], precision=lax.Precision.HIGHEST)
    open_slot = col == i
    for idx, _ in _top3_columns(jnp.where(col < i, gate, NEG_INF), col, LANES):
        open_slot = open_slot | (col == idx)
    qs = jnp.concatenate([qstack * (MOBA_HD ** -0.5), jnp.where(open_slot, 0.0, MASKED)], axis=1).astype(BF16)

    own = pl.ds(pl.multiple_of(i * bl, bl), bl)
    s = _dot_nt(qs, kb_ref[own, :]) + bown_ref[...]
    s = jnp.where(_iota((rows, bl), 1) <= _iota((rows, bl), 0) % bl, s, NEG_INF)
    carry = _attend_first(s, vb_ref[own, :])

    prev = pl.ds(pl.multiple_of(jnp.maximum(i - 1, 0) * bl, bl), bl)
    prev_mask = jnp.where(i == 0, MASKED, 0.0)
    carry = _attend_more(carry, _dot_nt(qs, kb_ref[prev, :]) + (bprev_ref[...] + prev_mask), vb_ref[prev, :])

    def body(n, carry):
        blk = pl.ds(pl.multiple_of(n * bl, bl), bl)
        return _attend_more(carry, _dot_nt(qs, kb_ref[blk, :]), vb_ref[blk, :])

    m, l, acc = lax.fori_loop(0, jnp.maximum(i - 1, 0), body, carry)
    o = acc / l
    oa, ob = _kv_lanes_to_head_cols([o[h * bl:(h + 1) * bl, :] for h in range(MOBA_HEADS)])
    o_ref[:, :LANES] = oa
    o_ref[:, LANES:] = ob


def _mla_prompt_kernel(q_ref, k_ref, o_ref, *, tq):
    i = pl.program_id(1)
    rows = MLA_HEADS * tq
    q = jnp.concatenate([q_ref[:, 2 * LANES * h:2 * LANES * (h + 1)] for h in range(MLA_HEADS)], axis=0)
    k_own = k_ref[pl.ds(pl.multiple_of(i * tq, tq), tq), :]
    s = jnp.where(_iota((rows, tq), 1) <= _iota((rows, tq), 0) % tq, _dot_nt(q, k_own), NEG_INF)
    carry = _attend_first(s, k_own[:, :MLA_LAT])

    def body(j, carry):
        kj = k_ref[pl.ds(pl.multiple_of(j * tq, tq), tq), :]
        return _attend_more(carry, _dot_nt(q, kj), kj[:, :MLA_LAT])

    m, l, acc = lax.fori_loop(0, i, body, carry)
    o = acc / l
    for h in range(MLA_HEADS):
        o_ref[:, MLA_LAT * h:MLA_LAT * (h + 1)] = o[h * tq:(h + 1) * tq, :]


def _page_copies(pt_ref, caches, bufs, sem_ref, b, slot, page_base, n_pages, page, do_start):
    def one(p, carry):
        src = page_base + pt_ref[b, p]
        for a, (cache_ref, (buf_ref, feature_major)) in enumerate(zip(caches, bufs)):
            span = pl.ds(pl.multiple_of(p * page, page), page)
            dst = buf_ref.at[slot, :, span] if feature_major else buf_ref.at[slot, span]
            cp = pltpu.make_async_copy(cache_ref.at[src], dst, sem_ref.at[slot, a])
            if do_start:
                cp.start(priority=a % 2)
            else:
                cp.wait()
        return carry
    lax.fori_loop(0, n_pages, one, 0)


def _prefetch_schedule(pt_ref, caches, bufs, sem_ref, page_base, n_pages, page):
    b = pl.program_id(0)
    nb = pl.num_programs(0)
    slot = b % 2
    args = (pt_ref, caches, bufs, sem_ref)

    @pl.when(b == 0)
    def _():
        _page_copies(*args, b, slot, page_base, n_pages, page, True)

    @pl.when(b + 1 < nb)
    def _():
        _page_copies(*args, b + 1, 1 - slot, page_base, n_pages, page, True)

    _page_copies(*args, b, slot, page_base, n_pages, page, False)
    return slot


def _moba_decode_kernel(pt_ref, rb_ref, q_ref, kn_ref, vn_ref, kc_hbm, vc_hbm, o_ref,
                        kbuf, vbuf, sem, s_ref, *, page_base, n_pages, page, ck, ls):
    slot = _prefetch_schedule(pt_ref, (kc_hbm, vc_hbm), ((kbuf, True), (vbuf, True)), sem,
                              page_base, n_pages, page)
    bl = MOBA_BLOCK
    nk = n_pages * page
    nbc = nk // bl
    rows = MOBA_HEADS * ls

    slot_lane = _iota((LANES, LANES), 1)
    kmt = jnp.zeros((LANES, LANES), F32)
    for n in range(nbc):
        mean_n = jnp.sum(kbuf[slot, :, n * bl:(n + 1) * bl], axis=1, keepdims=True) * (1.0 / bl)
        kmt = jnp.where(slot_lane == n, mean_n, kmt)

    qstack = jnp.concatenate(_head_rows_to_kv_lanes(q_ref[...]), axis=0)
    gate = jnp.dot(qstack, kmt, precision=lax.Precision.HIGHEST, preferred_element_type=F32)
    colg = _iota((rows, LANES), 1)
    picks = _top3_columns(jnp.where(colg < nbc, gate, NEG_INF), colg, LANES)

    rid = _iota((rows, 1), 0)
    head = rid // ls
    qi = rid % ls

    def rb_col(k):
        out = jnp.zeros((rows, 1), F32)
        for h in range(MOBA_HEADS):
            out = jnp.where(head == h, rb_ref[k, h], out)
        return out

    far_bias = rb_col(N_BUCKETS - 1)
    qs = (qstack * (MOBA_HD ** -0.5)).astype(BF16)
    m = jnp.full((rows, 1), NEG_INF, F32)
    n_chunks = nk // ck
    blk_in_chunk = _iota((rows, ck), 1) // bl
    picked = [jnp.where(ok, idx, -1) for idx, ok in picks]
    for c in range(n_chunks):
        s = _dot(qs, kbuf[slot, :, c * ck:(c + 1) * ck].astype(BF16))
        first = c * (ck // bl)
        chosen = blk_in_chunk == picked[0] - first
        for idx in picked[1:]:
            chosen = chosen | (blk_in_chunk == idx - first)
        s = jnp.where(chosen, s, NEG_INF)
        s_ref[:, c * ck:(c + 1) * ck] = s
        m = jnp.maximum(m, jnp.max(s, axis=-1, keepdims=True))
    last = nk - bl
    dist_last = (bl + qi) - _iota((rows, bl), 1)
    s_last = s_ref[:, last:] + (_t5_bias(dist_last, rb_col) - far_bias)
    s_ref[:, last:] = s_last
    m = jnp.maximum(m, jnp.max(s_last, axis=-1, keepdims=True))

    k_new = kn_ref[...]
    v_new = vn_ref[...]
    colo = _iota((rows, ls), 1)
    q32 = qstack * (MOBA_HD ** -0.5)
    s_own = jnp.zeros((rows, ls), F32)
    for j in range(ls):
        sj = jnp.sum(q32 * k_new[j:j + 1, :], axis=-1, keepdims=True)
        s_own = jnp.where(colo == j, sj, s_own)
    s_own = s_own + (_t5_bias(qi - colo, rb_col) - far_bias)
    s_own = jnp.where(colo <= qi, s_own, NEG_INF)
    m = jnp.maximum(m, jnp.max(s_own, axis=-1, keepdims=True))

    p_own = jnp.exp(s_own - m)
    l = jnp.sum(p_own, axis=-1, keepdims=True)
    acc = jnp.zeros((rows, LANES), F32)
    for j in range(ls):
        acc = acc + p_own[:, j:j + 1] * v_new[j:j + 1, :]
    for c in range(n_chunks):
        p = jnp.exp(s_ref[:, c * ck:(c + 1) * ck] - m)
        l = l + jnp.sum(p, axis=-1, keepdims=True)
        acc = acc + _dot_nt(p.astype(BF16), vbuf[slot, :, c * ck:(c + 1) * ck].astype(BF16))
    o = acc / l
    oa, ob = _kv_lanes_to_head_cols([o[h * ls:(h + 1) * ls, :] for h in range(MOBA_HEADS)])
    o_ref[:, :LANES] = oa
    o_ref[:, LANES:] = ob


def _mla_decode_kernel(pt_ref, q_ref, cn_ref, krn_ref, cc_hbm, rc_hbm, o_ref,
                       cbuf, rbuf, sem, s_ref, *, page_base, n_pages, page, ck, ls):
    slot = _prefetch_schedule(pt_ref, (cc_hbm, rc_hbm), ((cbuf, False), (rbuf, True)), sem,
                              page_base, n_pages, page)
    nk = n_pages * page
    rows = MLA_HEADS * ls
    q = q_ref[...]
    qlat = jnp.concatenate([q[:, 2 * LANES * h:2 * LANES * h + LANES] for h in range(MLA_HEADS)], axis=0)
    qrp = jnp.concatenate([q[:, 2 * LANES * h + LANES:2 * LANES * (h + 1)] for h in range(MLA_HEADS)], axis=0)
    qr = qrp[:, :MLA_ROPE]
    qi = _iota((rows, 1), 0) % ls

    m = jnp.full((rows, 1), NEG_INF, F32)
    n_chunks = nk // ck
    for c in range(n_chunks):
        s = (_dot_nt(qlat, cbuf[slot, c * ck:(c + 1) * ck, :].astype(BF16))
             + _dot(qr, rbuf[slot, :, c * ck:(c + 1) * ck].astype(BF16)))
        s_ref[:, c * ck:(c + 1) * ck] = s
        m = jnp.maximum(m, jnp.max(s, axis=-1, keepdims=True))

    c_new = cn_ref[...]
    kr_new = krn_ref[...]
    colo = _iota((rows, ls), 1)
    qlat32 = qlat.astype(F32)
    qrp32 = qrp.astype(F32)
    s_own = jnp.zeros((rows, ls), F32)
    for j in range(ls):
        sj = (jnp.sum(qlat32 * c_new[j:j + 1, :], axis=-1, keepdims=True)
              + jnp.sum(qrp32 * kr_new[j:j + 1, :], axis=-1, keepdims=True))
        s_own = jnp.where(colo == j, sj, s_own)
    s_own = jnp.where(colo <= qi, s_own, NEG_INF)
    m = jnp.maximum(m, jnp.max(s_own, axis=-1, keepdims=True))

    p_own = jnp.exp(s_own - m)
    l = jnp.sum(p_own, axis=-1, keepdims=True)
    acc = jnp.zeros((rows, MLA_LAT), F32)
    for j in range(ls):
        acc = acc + p_own[:, j:j + 1] * c_new[j:j + 1, :]
    for c in range(n_chunks):
        p = jnp.exp(s_ref[:, c * ck:(c + 1) * ck] - m)
        l = l + jnp.sum(p, axis=-1, keepdims=True)
        acc = acc + _dot(p.astype(BF16), cbuf[slot, c * ck:(c + 1) * ck, :].astype(BF16))
    o = acc / l
    for h in range(MLA_HEADS):
        o_ref[:, MLA_LAT * h:MLA_LAT * (h + 1)] = o[h * ls:(h + 1) * ls, :]


def _params(*sem):
    return pltpu.CompilerParams(dimension_semantics=sem, vmem_limit_bytes=VMEM_LIMIT_BYTES)


def _row_spec(tm, width):
    return pl.BlockSpec((tm, width), lambda i: (i, 0))


def _const_spec(shape):
    nd = len(shape)
    return pl.BlockSpec(shape, lambda i: (0,) * nd, pipeline_mode=pl.Buffered(1))


def _row_tile(t):
    tm = min(ROW_TILE, t)
    assert t % tm == 0 and tm % SUBLANES == 0, (t, tm)
    return tm


def _ffn_step(x, gpre, gpost, wg, wu, wd):
    t, d = x.shape
    tm = _row_tile(t)
    return pl.pallas_call(
        _ffn_kernel, grid=(t // tm,),
        in_specs=[_row_spec(tm, d), _const_spec(gpre.shape), _const_spec(gpost.shape),
                  _const_spec(wg.shape), _const_spec(wu.shape), _const_spec(wd.shape)],
        out_specs=_row_spec(tm, d), out_shape=jax.ShapeDtypeStruct((t, d), F32),
        compiler_params=_params("arbitrary"), name="ffn_step",
    )(x, gpre, gpost, wg, wu, wd)


def _ple_step(x, p, gpre, gpost, wpg, wpp):
    t, d = x.shape
    tm = _row_tile(t)
    return pl.pallas_call(
        _ple_kernel, grid=(t // tm,),
        in_specs=[_row_spec(tm, d), _row_spec(tm, p.shape[1]), _const_spec(gpre.shape), _const_spec(gpost.shape),
                  _const_spec(wpg.shape), _const_spec(wpp.shape)],
        out_specs=_row_spec(tm, d), out_shape=jax.ShapeDtypeStruct((t, d), F32),
        compiler_params=_params("arbitrary"), name="ple_step",
    )(x, p, gpre, gpost, wpg, wpp)


def _merge_step(x, o_moba, o_lat, o_conv, o_gmlp, gpre, gpost, wgate, wuv, wbr, wout):
    t, d = x.shape
    tm = _row_tile(t)
    return pl.pallas_call(
        _merge_kernel, grid=(t // tm,),
        in_specs=[_row_spec(tm, d), _row_spec(tm, o_moba.shape[1]), _row_spec(tm, o_lat.shape[1]),
                  _row_spec(tm, o_conv.shape[1]), _row_spec(tm, o_gmlp.shape[1]),
                  _const_spec(gpre.shape), _const_spec(gpost.shape), _const_spec(wgate.shape),
                  _const_spec(wuv.shape), _const_spec(wbr.shape), _const_spec(wout.shape)],
        out_specs=_row_spec(tm, d), out_shape=jax.ShapeDtypeStruct((t, d), F32),
        compiler_params=_params("arbitrary"), name="merge_step",
    )(x, o_moba, o_lat, o_conv, o_gmlp, gpre, gpost, wgate, wuv, wbr, wout)


def _mixer_pre(x, g, lw, cos_tab, sin_tab, st1, st2, wst, gbias, seq_len):
    t, d = x.shape
    tm = _row_tile(t)
    short_seq = seq_len < tm
    if short_seq:
        assert seq_len == SUBLANES and st1.shape == (t, 256), (seq_len, st1.shape)
        tiles_per_seq = 1
        st_spec = _row_spec(tm, 256)
        tab_spec = _const_spec(cos_tab.shape)
    else:
        assert seq_len % tm == 0
        tiles_per_seq = seq_len // tm
        st_spec = _const_spec(st1.shape)
        tab_spec = pl.BlockSpec((tm, LANES), lambda i: (i % tiles_per_seq, 0))
    ch = wst.shape[1]
    assert tm % ch == 0
    widths = [(256, F32), (128, F32), (128, F32), (256, BF16), (128, BF16), (128, F32), (128, F32),
              (1024, BF16), (256, BF16), (256, F32), (256, F32), (256, F32), (256, F32)]
    kern = functools.partial(_mixer_pre_kernel, tm=tm, short_seq=short_seq, tiles_per_seq=tiles_per_seq, ch=ch)
    return pl.pallas_call(
        kern, grid=(t // tm,),
        in_specs=[_row_spec(tm, d), _const_spec(g.shape), _const_spec(lw["w_in"].shape),
                  _const_spec(lw["qn"].shape), _const_spec(lw["kvn"].shape), _const_spec(lw["wq"].shape),
                  _const_spec(lw["wuk"].shape), tab_spec, tab_spec, _const_spec(lw["conv_w"].shape),
                  st_spec, st_spec, _const_spec(wst.shape), _const_spec(gbias.shape)],
        out_specs=[_row_spec(tm, w) for w, _ in widths],
        out_shape=[jax.ShapeDtypeStruct((t, w), dt) for w, dt in widths],
        scratch_shapes=[pltpu.VMEM((SUBLANES, 256), F32)],
        compiler_params=_params("arbitrary"), name="mixer_pre",
    )(x, g, lw["w_in"], lw["qn"], lw["kvn"], lw["wq"], lw["wuk"], cos_tab, sin_tab, lw["conv_w"],
      st1, st2, wst, gbias)


def _moba_prompt(rel_bias, q, k, kb, vb, batch, seq):
    bl = MOBA_BLOCK
    assert seq % bl == 0 and seq // bl <= LANES
    nb = seq // bl
    return pl.pallas_call(
        functools.partial(_moba_prompt_kernel, nb=nb), grid=(batch, nb),
        in_specs=[pl.BlockSpec(memory_space=pltpu.SMEM),
                  pl.BlockSpec((bl, 256), lambda b, i: (b * nb + i, 0)),
                  pl.BlockSpec((seq, 128), lambda b, i: (b, 0)),
                  pl.BlockSpec((seq, 256), lambda b, i: (b, 0)),
                  pl.BlockSpec((seq, 128), lambda b, i: (b, 0))],
        out_specs=pl.BlockSpec((bl, 256), lambda b, i: (b * nb + i, 0)),
        out_shape=jax.ShapeDtypeStruct((batch * seq, 256), F32),
        scratch_shapes=[pltpu.VMEM((LANES, LANES), F32),
                        pltpu.VMEM((MOBA_HEADS * bl, bl), F32), pltpu.VMEM((MOBA_HEADS * bl, bl), F32)],
        compiler_params=_params("arbitrary", "arbitrary"), name="moba_prompt",
    )(rel_bias, q, k, kb, vb)


def _mla_prompt(qcat, kcat, batch, seq):
    tq = min(ATT_TILE, seq)
    assert seq % tq == 0
    nq = seq // tq
    return pl.pallas_call(
        functools.partial(_mla_prompt_kernel, tq=tq), grid=(batch, nq),
        in_specs=[pl.BlockSpec((tq, qcat.shape[1]), lambda b, i: (b * nq + i, 0)),
                  pl.BlockSpec((seq, kcat.shape[1]), lambda b, i: (b, 0))],
        out_specs=pl.BlockSpec((tq, MLA_HEADS * MLA_LAT), lambda b, i: (b * nq + i, 0)),
        out_shape=jax.ShapeDtypeStruct((batch * seq, MLA_HEADS * MLA_LAT), F32),
        compiler_params=_params("arbitrary", "arbitrary"), name="mla_prompt",
    )(qcat, kcat)


def _decode_geometry(page_table, page, ls):
    bd, n_pages = page_table.shape
    nk = n_pages * page
    assert nk % MOBA_BLOCK == 0 and nk // MOBA_BLOCK <= LANES and ls == SUBLANES and page % LANES == 0, (nk, ls, page)
    ck = min(DEC_KEY_CHUNK, nk)
    assert nk % ck == 0
    return bd, n_pages, nk, ck


def _feature_major_pages(cache):
    depth, pool, page = cache.shape[:3]
    nd = cache.ndim
    return jnp.transpose(cache, (0, 1) + tuple(range(3, nd)) + (2,)).reshape(depth * pool, -1, page)


def _moba_decode(page_table, rel_bias, q, k_new, v_new, cache_k, cache_v, layer, ls):
    n_pool, page = cache_k.shape[1], cache_k.shape[2]
    bd, n_pages, nk, ck = _decode_geometry(page_table, page, ls)
    kc = _feature_major_pages(cache_k)
    vc = _feature_major_pages(cache_v)
    rows = MOBA_HEADS * ls
    kern = functools.partial(_moba_decode_kernel, page_base=layer * n_pool, n_pages=n_pages, page=page, ck=ck, ls=ls)
    grid_spec = pltpu.PrefetchScalarGridSpec(
        num_scalar_prefetch=1, grid=(bd,),
        in_specs=[pl.BlockSpec(memory_space=pltpu.SMEM),
                  pl.BlockSpec((ls, 256), lambda b, pt: (b, 0)),
                  pl.BlockSpec((ls, 128), lambda b, pt: (b, 0)),
                  pl.BlockSpec((ls, 128), lambda b, pt: (b, 0)),
                  pl.BlockSpec(memory_space=pl.ANY), pl.BlockSpec(memory_space=pl.ANY)],
        out_specs=pl.BlockSpec((ls, 256), lambda b, pt: (b, 0)),
        scratch_shapes=[pltpu.VMEM((2, 128, nk), F32), pltpu.VMEM((2, 128, nk), F32),
                        pltpu.SemaphoreType.DMA((2, 2)), pltpu.VMEM((rows, nk), F32)])
    return pl.pallas_call(
        kern, grid_spec=grid_spec, out_shape=jax.ShapeDtypeStruct((bd * ls, 256), F32),
        compiler_params=_params("arbitrary"), name="moba_decode",
    )(page_table, rel_bias, q, k_new, v_new, kc, vc)


def _mla_decode(page_table, qcat, c_new, kr_new, cache_lat, cache_rope, layer, ls):
    n_pool, page = cache_lat.shape[1], cache_lat.shape[2]
    bd, n_pages, nk, ck = _decode_geometry(page_table, page, ls)
    cc = cache_lat.reshape(-1, page, MLA_LAT)
    rc = _feature_major_pages(cache_rope)
    rows = MLA_HEADS * ls
    kern = functools.partial(_mla_decode_kernel, page_base=layer * n_pool, n_pages=n_pages, page=page, ck=ck, ls=ls)
    grid_spec = pltpu.PrefetchScalarGridSpec(
        num_scalar_prefetch=1, grid=(bd,),
        in_specs=[pl.BlockSpec((ls, qcat.shape[1]), lambda b, pt: (b, 0)),
                  pl.BlockSpec((ls, 128), lambda b, pt: (b, 0)),
                  pl.BlockSpec((ls, 128), lambda b, pt: (b, 0)),
                  pl.BlockSpec(memory_space=pl.ANY), pl.BlockSpec(memory_space=pl.ANY)],
        out_specs=pl.BlockSpec((ls, MLA_HEADS * MLA_LAT), lambda b, pt: (b, 0)),
        scratch_shapes=[pltpu.VMEM((2, nk, MLA_LAT), F32), pltpu.VMEM((2, MLA_ROPE, nk), F32),
                        pltpu.SemaphoreType.DMA((2, 2)), pltpu.VMEM((rows, nk), F32)])
    return pl.pallas_call(
        kern, grid_spec=grid_spec, out_shape=jax.ShapeDtypeStruct((bd * ls, MLA_HEADS * MLA_LAT), F32),
        compiler_params=_params("arbitrary"), name="mla_decode",
    )(page_table, qcat, c_new, kr_new, cc, rc)


def _prep_layer(i, norm_g, w_ffn_gate, w_ffn_up, w_ffn_down, w_in, w_gate, mla_q_norm, mla_kv_norm,
                w_mla_q_b, w_mla_kv_b, conv_w, w_branch, w_out, w_ple_gate, w_ple_proj):
    d = w_in.shape[1]
    wi = w_in[i]
    mkr_lo, mkr_hi = 896, 928
    w_in_p = jnp.concatenate([wi[:, :mkr_lo], wi[:, mkr_hi:], wi[:, mkr_lo:mkr_hi],
                              jnp.zeros((d, LANES - MLA_ROPE), F32)], axis=1)
    assert w_in_p.shape[1] == C_END
    lat_q = w_mla_q_b.shape[1]
    wq4 = w_mla_q_b[i].reshape(lat_q, MLA_HEADS, MLA_NOPE + MLA_ROPE)
    wq_rope = jnp.pad(wq4[:, :, MLA_NOPE:], ((0, 0), (0, 0), (0, LANES - MLA_ROPE)))
    wq = jnp.concatenate([wq4[:, :, :MLA_NOPE].reshape(lat_q, -1), wq_rope.reshape(lat_q, -1)], axis=1)
    eye = jnp.eye(MLA_HEADS, dtype=F32)
    wkv = w_mla_kv_b[i]
    wuk = jnp.einsum("chn,hg->hngc", wkv[..., :MLA_NOPE], eye).reshape(MLA_HEADS * MLA_NOPE, MLA_HEADS * MLA_LAT)
    wuv = jnp.einsum("chv,hg->hcgv", wkv[..., MLA_NOPE:], eye).reshape(MLA_HEADS * MLA_LAT, -1)
    return {
        "g": norm_g[i][:, None, :],
        "wfg": w_ffn_gate[i].astype(BF16), "wfu": w_ffn_up[i].astype(BF16), "wfd": w_ffn_down[i].astype(BF16),
        "w_in": w_in_p.astype(BF16), "w_gate": w_gate[i].astype(BF16),
        "qn": mla_q_norm[i][None, :], "kvn": mla_kv_norm[i][None, :],
        "wq": wq.astype(BF16), "wuk": wuk.astype(BF16), "wuv": wuv.astype(BF16),
        "conv_w": conv_w[i], "w_branch": w_branch[i].astype(BF16), "w_out": w_out[i].astype(BF16),
        "wpg": w_ple_gate[i].astype(BF16), "wpp": w_ple_proj[i].astype(BF16),
    }


def _rope_tables(pos):
    half = MLA_ROPE // 2
    freq = ROPE_THETA ** (-jnp.arange(half, dtype=F32) / half)
    ang = pos.astype(F32)[:, None] * freq
    cos, sin = jnp.cos(ang), jnp.sin(ang)
    pad = jnp.zeros((pos.shape[0], LANES - MLA_ROPE), F32)
    return jnp.concatenate([cos, cos, pad], axis=1), jnp.concatenate([-sin, sin, pad], axis=1)


def _gmlp_tables(ws, bs, seq_len, ch):
    n = min(seq_len, ch)
    w = (ws * jnp.tril(jnp.ones((GMLP_CHUNK, GMLP_CHUNK), ws.dtype)))[:, :n, :n]
    b = jnp.repeat(jnp.transpose(bs)[:n], 256 // GMLP_GROUPS, axis=1)
    reps = ch // n
    if reps > 1:
        w = jnp.einsum("gts,ab->gatbs", w, jnp.eye(reps, dtype=ws.dtype)).reshape(GMLP_GROUPS, ch, ch)
        b = jnp.tile(b, (reps, 1))
    return w.reshape(GMLP_GROUPS * ch, ch).astype(BF16), b


def _layer(x, p, lw, rel_bias, tabs, attend):
    g = lw["g"]
    x = _ffn_step(x, g[0], g[1], lw["wfg"][0], lw["wfu"][0], lw["wfd"][0])
    (q, k, v, kb, vb, c_new, kr, qcat, kcat, o_conv, o_gmlp, vg, hc) = _mixer_pre(
        x, g[2], lw, tabs["cos"], tabs["sin"], tabs["st1"], tabs["st2"], tabs["wst"], tabs["gbias"], tabs["seq_len"])
    o_moba, o_lat = attend(q, k, v, kb, vb, c_new, kr, qcat, kcat)
    x = _merge_step(x, o_moba, o_lat, o_conv, o_gmlp, g[2], g[3], lw["w_gate"], lw["wuv"], lw["w_branch"], lw["w_out"])
    x = _ffn_step(x, g[4], g[5], lw["wfg"][1], lw["wfu"][1], lw["wfd"][1])
    x = _ple_step(x, p, g[6], g[7], lw["wpg"], lw["wpp"])
    return x, (k, v, c_new, kr[:, :MLA_ROPE], hc, vg)


def kernel(x_prompt, x_sample, cache_moba_k, cache_moba_v, cache_mla_latent, cache_mla_rope, state_conv, page_table, p_prompt, p_sample, rel_bias, norm_g, w_ffn_gate, w_ffn_up, w_ffn_down, w_in, w_gate, mla_q_norm, mla_kv_norm, w_mla_q_b, w_mla_kv_b, conv_w, gmlp_ws, gmlp_b, w_branch, w_out, w_ple_gate, w_ple_proj):
    depth = norm_g.shape[0]
    bp, lp, d = x_prompt.shape
    bd, ls, _ = x_sample.shape
    page = cache_moba_k.shape[2]
    past_len = page_table.shape[1] * page
    tp, ts = bp * lp, bd * ls

    cos_p, sin_p = _rope_tables(jnp.arange(lp, dtype=jnp.int32))
    cos_s, sin_s = _rope_tables(past_len + jnp.arange(ls, dtype=jnp.int32))
    tm_s = _row_tile(ts)
    cos_s, sin_s = jnp.tile(cos_s, (tm_s // ls, 1)), jnp.tile(sin_s, (tm_s // ls, 1))
    ch_p = min(GMLP_CHUNK, _row_tile(tp))
    ch_s = min(GMLP_CHUNK, tm_s)
    no_state = jnp.zeros((SUBLANES, 256), F32)

    yp = x_prompt.reshape(tp, d)
    ys = x_sample.reshape(ts, d)
    st_p, st_s = [], []
    for i in range(depth):
        lw = _prep_layer(i, norm_g, w_ffn_gate, w_ffn_up, w_ffn_down, w_in, w_gate, mla_q_norm, mla_kv_norm,
                         w_mla_q_b, w_mla_kv_b, conv_w, w_branch, w_out, w_ple_gate, w_ple_proj)
        wst_p, gb_p = _gmlp_tables(gmlp_ws[i], gmlp_b[i], lp, ch_p)
        wst_s, gb_s = _gmlp_tables(gmlp_ws[i], gmlp_b[i], ls, ch_s)
        prev = state_conv[i]
        st1 = jnp.pad(prev[:, 1:2], ((0, 0), (0, ls - 1), (0, 0))).reshape(ts, 256)
        st2 = jnp.pad(prev, ((0, 0), (0, ls - 2), (0, 0))).reshape(ts, 256)
        tabs_p = dict(cos=cos_p, sin=sin_p, st1=no_state, st2=no_state, wst=wst_p, gbias=gb_p, seq_len=lp)
        tabs_s = dict(cos=cos_s, sin=sin_s, st1=st1, st2=st2, wst=wst_s, gbias=gb_s, seq_len=ls)

        def attend_prompt(q, k, v, kb, vb, c_new, kr, qcat, kcat):
            return _moba_prompt(rel_bias, q, k, kb, vb, bp, lp), _mla_prompt(qcat, kcat, bp, lp)

        def attend_sample(q, k, v, kb, vb, c_new, kr, qcat, kcat, i=i):
            return (_moba_decode(page_table, rel_bias, q, k, v, cache_moba_k, cache_moba_v, i, ls),
                    _mla_decode(page_table, qcat, c_new, kr, cache_mla_latent, cache_mla_rope, i, ls))

        yp, sp = _layer(yp, p_prompt[i].reshape(tp, -1), lw, rel_bias, tabs_p, attend_prompt)
        ys, ss = _layer(ys, p_sample[i].reshape(ts, -1), lw, rel_bias, tabs_s, attend_sample)
        st_p.append(sp)
        st_s.append(ss)

    def stack(states, j, shape):
        return jnp.stack([s[j].reshape(shape) for s in states], axis=0)

    kv_p, kv_s = (bp, lp, 2, MOBA_HD), (bd, ls, 2, MOBA_HD)
    conv_p = jnp.stack([s[4].reshape(bp, lp, 256)[:, -2:] for s in st_p], axis=0)
    conv_s = jnp.stack([s[4].reshape(bd, ls, 256)[:, -2:] for s in st_s], axis=0)
    return (yp.reshape(bp, lp, d), ys.reshape(bd, ls, d),
            stack(st_p, 0, kv_p), stack(st_p, 1, kv_p), stack(st_p, 2, (bp, lp, MLA_LAT)),
            stack(st_p, 3, (bp, lp, MLA_ROPE)), conv_p,
            stack(st_s, 0, kv_s), stack(st_s, 1, kv_s), stack(st_s, 2, (bd, ls, MLA_LAT)),
            stack(st_s, 3, (bd, ls, MLA_ROPE)), conv_s, stack(st_s, 5, (bd, ls, 256)))
```

```python
import functools
import math

import jax
import jax.numpy as jnp
from jax import lax
from jax.experimental import pallas as pl
from jax.experimental.pallas import tpu as pltpu

F32 = jnp.float32
BF16 = jnp.bfloat16
EPS = 1e-6
NEG_INF = float("-inf")
MASKED = -1e30

N_BRANCH = 4
MOBA_HEADS = 4
MOBA_HD = 64
MOBA_BLOCK = 256
MOBA_TOPK = 3
MLA_HEADS = 4
MLA_NOPE = 64
MLA_ROPE = 32
MLA_LAT = 128
ROPE_THETA = 10000.0
GMLP_GROUPS = 4
GMLP_CHUNK = 128
N_BUCKETS = 32
T5_MAX_DIST = 128
T5_EXACT = N_BUCKETS // 2

LANES = 128
SUBLANES = 8
VMEM_LIMIT_BYTES = 56 * 1024 * 1024

ROW_TILE = 512
ATT_TILE = 256
DEC_KEY_CHUNK = 4096

C_MQ, C_MK, C_MV, C_MQA, C_MKV, C_CB, C_CC, C_CX, C_GU, C_GV, C_MKR, C_END = (
    0, 256, 384, 512, 768, 896, 1152, 1408, 1664, 1920, 2176, 2304)

NT_DIMS = (((1,), (1,)), ((), ()))


def _rms(x, g):
    return x * lax.rsqrt(jnp.mean(x * x, axis=-1, keepdims=True) + EPS) * g


def _dot(a, b):
    return jnp.dot(a, b, preferred_element_type=F32)


def _dot_nt(a, b, precision=None):
    return lax.dot_general(a, b, NT_DIMS, precision=precision, preferred_element_type=F32)


def _iota(shape, dim):
    return lax.broadcasted_iota(jnp.int32, shape, dim)


def _t5_bias(dist, rb_of_bucket):
    n = jnp.maximum(dist, 0)
    nf = jnp.maximum(n, T5_EXACT).astype(F32)
    large = T5_EXACT + (jnp.log(nf / T5_EXACT) / math.log(T5_MAX_DIST / T5_EXACT)
                        * (N_BUCKETS - T5_EXACT)).astype(jnp.int32)
    bucket = jnp.where(n < T5_EXACT, n, jnp.minimum(large, N_BUCKETS - 1))
    out = jnp.zeros(dist.shape, F32)
    for k in range(N_BUCKETS):
        out = jnp.where(bucket == k, rb_of_bucket(k), out)
    return out


def _head_rows_to_kv_lanes(q):
    qa, qb = q[:, :LANES], q[:, LANES:]
    lo = _iota(qa.shape, 1) < MOBA_HD
    zero = jnp.zeros_like(qa)
    return [jnp.where(lo, qa, zero), jnp.where(lo, pltpu.roll(qa, MOBA_HD, 1), zero),
            jnp.where(lo, zero, pltpu.roll(qb, MOBA_HD, 1)), jnp.where(lo, zero, qb)]


def _kv_lanes_to_head_cols(o):
    lo = _iota(o[0].shape, 1) < MOBA_HD
    return (jnp.where(lo, o[0], pltpu.roll(o[1], MOBA_HD, 1)),
            jnp.where(lo, pltpu.roll(o[2], MOBA_HD, 1), o[3]))


def _top3_columns(work, col, ncol):
    picks = []
    colf = col.astype(F32)
    for _ in range(MOBA_TOPK):
        mx = jnp.max(work, axis=-1, keepdims=True)
        ok = mx > NEG_INF
        idx = jnp.min(jnp.where((work == mx) & ok, colf, float(ncol)), axis=-1, keepdims=True).astype(jnp.int32)
        work = jnp.where(col == idx, NEG_INF, work)
        picks.append((idx, ok))
    return picks


def _ffn_kernel(x_ref, gpre_ref, gpost_ref, wg_ref, wu_ref, wd_ref, o_ref):
    x = x_ref[...]
    xn = _rms(x, gpre_ref[...]).astype(BF16)
    g = _dot(xn, wg_ref[...])
    u = _dot(xn, wu_ref[...])
    a = (jax.nn.silu(g) * u).astype(BF16)
    y = _dot(a, wd_ref[...])
    o_ref[...] = x + 0.5 * _rms(y, gpost_ref[...])


def _ple_kernel(x_ref, p_ref, gpre_ref, gpost_ref, wpg_ref, wpp_ref, o_ref):
    x = x_ref[...]
    hp = _rms(x, gpre_ref[...]).astype(BF16)
    gate = jax.nn.sigmoid(_dot(hp, wpg_ref[...]))
    pp = _dot(p_ref[...].astype(BF16), wpp_ref[...])
    o_ref[...] = x + _rms(gate * pp, gpost_ref[...])


def _merge_kernel(x_ref, om_ref, ol_ref, oc_ref, og_ref, gpre_ref, gpost_ref,
                  wgate_ref, wuv_ref, wbr_ref, wout_ref, o_ref):
    x = x_ref[...]
    d = x.shape[1]
    h = _rms(x, gpre_ref[...]).astype(BF16)
    o_mla = _dot(ol_ref[...].astype(BF16), wuv_ref[...])
    branches = (om_ref[...], o_mla, oc_ref[...], og_ref[...])
    acc = None
    for n in range(N_BRANCH):
        gate = jax.nn.sigmoid(_dot(h, wgate_ref[:, n * d:(n + 1) * d]))
        term = gate * _dot(branches[n].astype(BF16), wbr_ref[n])
        acc = term if acc is None else acc + term
    out = _dot(acc.astype(BF16), wout_ref[...])
    o_ref[...] = x + _rms(out, gpost_ref[...])


def _mixer_pre_kernel(x_ref, g_ref, win_ref, qn_ref, kvn_ref, wq_ref, wuk_ref, cos_ref, sin_ref,
                      cw_ref, st1_ref, st2_ref, wst_ref, gb_ref,
                      q_o, k_o, v_o, kb_o, vb_o, c_o, kr_o, qcat_o, kcat_o, oc_o, og_o, vg_o, hc_o,
                      carry_ref, *, tm, short_seq, tiles_per_seq, ch):
    x = x_ref[...]
    h = _rms(x, g_ref[...]).astype(BF16)
    proj = _dot(h, win_ref[...])

    mk = proj[:, C_MK:C_MV]
    mv = proj[:, C_MV:C_MQA]
    q_o[...] = proj[:, C_MQ:C_MK]
    k_o[...] = mk
    v_o[...] = mv
    kb_o[:, :LANES] = mk.astype(BF16)
    if short_seq:
        kb_o[:, LANES:] = jnp.zeros((tm, LANES), BF16)
    else:
        pos = (pl.program_id(0) % tiles_per_seq) * tm + _iota((tm, LANES), 0)
        kb_o[:, LANES:] = jnp.where(pos // MOBA_BLOCK == _iota((tm, LANES), 1), 1.0, 0.0).astype(BF16)
    vb_o[...] = mv.astype(BF16)

    cos = cos_ref[...]
    sin = sin_ref[...]
    lane = _iota((tm, LANES), 1)
    half = MLA_ROPE // 2

    def rope(v):
        swapped = jnp.where(lane < half, pltpu.roll(v, LANES - half, 1), pltpu.roll(v, half, 1))
        return v * cos + swapped * sin

    cq = _rms(proj[:, C_MQA:C_MKV], qn_ref[...]).astype(BF16)
    qh = _dot(cq, wq_ref[...])
    nope_w = MLA_HEADS * MLA_NOPE
    qlat = _dot(qh[:, :nope_w].astype(BF16), wuk_ref[...])
    scale = (MLA_NOPE + MLA_ROPE) ** -0.5
    for hh in range(MLA_HEADS):
        qcat_o[:, 2 * LANES * hh:2 * LANES * hh + LANES] = (qlat[:, LANES * hh:LANES * (hh + 1)] * scale).astype(BF16)
        qr = rope(qh[:, nope_w + LANES * hh:nope_w + LANES * (hh + 1)])
        qcat_o[:, 2 * LANES * hh + LANES:2 * LANES * (hh + 1)] = (qr * scale).astype(BF16)
    c_new = _rms(proj[:, C_MKV:C_CB], kvn_ref[...])
    kr = rope(proj[:, C_MKR:C_END])
    c_o[...] = c_new
    kr_o[...] = kr
    kcat_o[:, :LANES] = c_new.astype(BF16)
    kcat_o[:, LANES:] = kr.astype(BF16)

    hc = proj[:, C_CC:C_CX] * proj[:, C_CX:C_GU]
    r1 = pltpu.roll(hc, 1, 0)
    r2 = pltpu.roll(hc, 2, 0)
    row = _iota(hc.shape, 0)
    if short_seq:
        rs = row % SUBLANES
        s1 = jnp.where(rs < 1, st1_ref[...], r1)
        s2 = jnp.where(rs < 2, st2_ref[...], r2)
    else:
        @pl.when(pl.program_id(0) % tiles_per_seq == 0)
        def _():
            carry_ref[...] = jnp.zeros(carry_ref.shape, F32)
        prev = carry_ref[...]
        s1 = jnp.where(row < 1, prev[SUBLANES - 1:SUBLANES, :], r1)
        s2 = jnp.where(row == 0, prev[SUBLANES - 2:SUBLANES - 1, :],
                       jnp.where(row == 1, prev[SUBLANES - 1:SUBLANES, :], r2))
        carry_ref[...] = hc[tm - SUBLANES:, :]
    cw = cw_ref[...]
    conv = cw[0:1, :] * s2 + cw[1:2, :] * s1 + cw[2:3, :] * hc
    oc_o[...] = proj[:, C_CB:C_CC] * conv
    hc_o[...] = hc

    u = jax.nn.gelu(proj[:, C_GU:C_GV])
    vg = jax.nn.gelu(proj[:, C_GV:C_MKR])
    vg_o[...] = vg
    wst = wst_ref[...]
    gb = gb_ref[...]
    group = _iota((ch, gb.shape[1]), 1) // (gb.shape[1] // GMLP_GROUPS)
    for c in range(tm // ch):
        mixed_all = _dot(wst, vg[c * ch:(c + 1) * ch, :].astype(BF16))
        mixed = gb
        for g in range(GMLP_GROUPS):
            mixed = mixed + jnp.where(group == g, mixed_all[g * ch:(g + 1) * ch, :], 0.0)
        og_o[c * ch:(c + 1) * ch, :] = u[c * ch:(c + 1) * ch, :] * mixed


def _attend_first(s, v):
    m = jnp.max(s, axis=-1, keepdims=True)
    p = jnp.exp(s - m)
    return m, jnp.sum(p, axis=-1, keepdims=True), _dot(p.astype(BF16), v)


def _attend_more(carry, s, v):
    m, l, acc = carry
    m_new = jnp.maximum(m, jnp.max(s, axis=-1, keepdims=True))
    alpha = jnp.exp(m - m_new)
    p = jnp.exp(s - m_new)
    return m_new, alpha * l + jnp.sum(p, axis=-1, keepdims=True), alpha * acc + _dot(p.astype(BF16), v)


def _moba_prompt_kernel(rb_ref, q_ref, k_ref, kb_ref, vb_ref, o_ref, km_ref, bown_ref, bprev_ref, *, nb):
    b = pl.program_id(0)
    i = pl.program_id(1)
    bl = MOBA_BLOCK
    rows = MOBA_HEADS * bl

    @pl.when((b == 0) & (i == 0))
    def _():
        dist = _iota((bl, bl), 0) - _iota((bl, bl), 1)
        for h in range(MOBA_HEADS):
            far = rb_ref[N_BUCKETS - 1, h]
            bown_ref[h * bl:(h + 1) * bl, :] = _t5_bias(dist, lambda k, h=h: rb_ref[k, h]) - far
            bprev_ref[h * bl:(h + 1) * bl, :] = _t5_bias(dist + bl, lambda k, h=h: rb_ref[k, h]) - far

    @pl.when(i == 0)
    def _():
        km_ref[...] = jnp.zeros(km_ref.shape, F32)
        for n in range(nb):
            km_ref[n:n + 1, :] = jnp.sum(k_ref[n * bl:(n + 1) * bl, :], axis=0, keepdims=True) * (1.0 / bl)

    qstack = jnp.concatenate(_head_rows_to_kv_lanes(q_ref[...]), axis=0)
    col = _iota((rows, LANES), 1)
    gate = _dot_nt(qstack, km_ref[...], precision=lax.Precision.HIGHEST)
    open_slot = col == i
    for idx, _ in _top3_columns(jnp.where(col < i, gate, NEG_INF), col, LANES):
        open_slot = open_slot | (col == idx)
    qs = jnp.concatenate([qstack * (MOBA_HD ** -0.5), jnp.where(open_slot, 0.0, MASKED)], axis=1).astype(BF16)

    own = pl.ds(pl.multiple_of(i * bl, bl), bl)
    s = _dot_nt(qs, kb_ref[own, :]) + bown_ref[...]
    s = jnp.where(_iota((rows, bl), 1) <= _iota((rows, bl), 0) % bl, s, NEG_INF)
    carry = _attend_first(s, vb_ref[own, :])

    prev = pl.ds(pl.multiple_of(jnp.maximum(i - 1, 0) * bl, bl), bl)
    prev_mask = jnp.where(i == 0, MASKED, 0.0)
    carry = _attend_more(carry, _dot_nt(qs, kb_ref[prev, :]) + (bprev_ref[...] + prev_mask), vb_ref[prev, :])

    def body(n, carry):
        blk = pl.ds(pl.multiple_of(n * bl, bl), bl)
        return _attend_more(carry, _dot_nt(qs, kb_ref[blk, :]), vb_ref[blk, :])

    m, l, acc = lax.fori_loop(0, jnp.maximum(i - 1, 0), body, carry)
    o = acc / l
    oa, ob = _kv_lanes_to_head_cols([o[h * bl:(h + 1) * bl, :] for h in range(MOBA_HEADS)])
    o_ref[:, :LANES] = oa
    o_ref[:, LANES:] = ob


def _mla_prompt_kernel(q_ref, k_ref, o_ref, *, tq):
    i = pl.program_id(1)
    rows = MLA_HEADS * tq
    q = jnp.concatenate([q_ref[:, 2 * LANES * h:2 * LANES * (h + 1)] for h in range(MLA_HEADS)], axis=0)
    k_own = k_ref[pl.ds(pl.multiple_of(i * tq, tq), tq), :]
    s = jnp.where(_iota((rows, tq), 1) <= _iota((rows, tq), 0) % tq, _dot_nt(q, k_own), NEG_INF)
    carry = _attend_first(s, k_own[:, :MLA_LAT])

    def body(j, carry):
        kj = k_ref[pl.ds(pl.multiple_of(j * tq, tq), tq), :]
        return _attend_more(carry, _dot_nt(q, kj), kj[:, :MLA_LAT])

    m, l, acc = lax.fori_loop(0, i, body, carry)
    o = acc / l
    for h in range(MLA_HEADS):
        o_ref[:, MLA_LAT * h:MLA_LAT * (h + 1)] = o[h * tq:(h + 1) * tq, :]


def _stream_pages(pt_ref, streams, sem_ref, page_base, n_pages, page):
    b = pl.program_id(0)
    last = pl.num_programs(0) - 1
    slot = b % 2
    nxt = jnp.minimum(b + 1, last)

    def copies(elem, sl, p):
        src = 0 if elem is None else page_base + pt_ref[elem, p]
        span = pl.ds(p * page, page)
        return [pltpu.make_async_copy(cache.at[src], buf.at[sl, :, span] if feature_major else buf.at[sl, span],
                                      sem_ref.at[sl, a])
                for a, (cache, buf, feature_major) in enumerate(streams)]

    def start_all(elem, sl):
        for p in range(n_pages):
            for a, cp in enumerate(copies(elem, sl, p)):
                cp.start(priority=a % 2)

    def wait_all(sl):
        for a in range(len(streams)):
            for p in range(n_pages):
                copies(None, sl, p)[a].wait()

    @pl.when(b == 0)
    def _():
        start_all(b, slot)

    wait_all(slot)
    start_all(nxt, 1 - slot)

    def drain():
        @pl.when(b == last)
        def _():
            wait_all(1 - slot)

    return slot, drain


def _moba_decode_kernel(pt_ref, rb_ref, q_ref, kn_ref, vn_ref, kc_hbm, vc_hbm, o_ref,
                        kbuf, vbuf, sem, s_ref, *, page_base, n_pages, page, ck, ls):
    slot, drain = _stream_pages(pt_ref, ((kc_hbm, kbuf, True), (vc_hbm, vbuf, True)), sem,
                                page_base, n_pages, page)
    bl = MOBA_BLOCK
    nk = n_pages * page
    nbc = nk // bl
    rows = MOBA_HEADS * ls

    qstack = jnp.concatenate(_head_rows_to_kv_lanes(q_ref[...]), axis=0)
    qs = (qstack * (MOBA_HD ** -0.5)).astype(BF16)

    slot_lane = _iota((LANES, LANES), 1)
    kmt = jnp.zeros((LANES, LANES), F32)
    for n in range(nbc):
        mean_n = jnp.sum(kbuf[slot, :, n * bl:(n + 1) * bl], axis=1, keepdims=True) * (1.0 / bl)
        kmt = jnp.where(slot_lane == n, mean_n, kmt)

    gate = jnp.dot(qstack, kmt, precision=lax.Precision.HIGHEST, preferred_element_type=F32)
    colg = _iota((rows, LANES), 1)
    picks = _top3_columns(jnp.where(colg < nbc, gate, NEG_INF), colg, LANES)

    rid = _iota((rows, 1), 0)
    head = rid // ls
    qi = rid % ls

    def rb_col(k):
        out = jnp.zeros((rows, 1), F32)
        for h in range(MOBA_HEADS):
            out = jnp.where(head == h, rb_ref[k, h], out)
        return out

    far_bias = rb_col(N_BUCKETS - 1)
    m = jnp.full((rows, 1), NEG_INF, F32)
    n_chunks = nk // ck
    blk_in_chunk = _iota((rows, ck), 1) // bl
    picked = [jnp.where(ok, idx, -1) for idx, ok in picks]
    for c in range(n_chunks):
        s = _dot(qs, kbuf[slot, :, c * ck:(c + 1) * ck].astype(BF16))
        first = c * (ck // bl)
        chosen = blk_in_chunk == picked[0] - first
        for idx in picked[1:]:
            chosen = chosen | (blk_in_chunk == idx - first)
        s = jnp.where(chosen, s, NEG_INF)
        s_ref[:, c * ck:(c + 1) * ck] = s
        m = jnp.maximum(m, jnp.max(s, axis=-1, keepdims=True))
    last = nk - bl
    dist_last = (bl + qi) - _iota((rows, bl), 1)
    s_last = s_ref[:, last:] + (_t5_bias(dist_last, rb_col) - far_bias)
    s_ref[:, last:] = s_last
    m = jnp.maximum(m, jnp.max(s_last, axis=-1, keepdims=True))

    k_new = kn_ref[...]
    v_new = vn_ref[...]
    colo = _iota((rows, ls), 1)
    q32 = qstack * (MOBA_HD ** -0.5)
    s_own = jnp.zeros((rows, ls), F32)
    for j in range(ls):
        sj = jnp.sum(q32 * k_new[j:j + 1, :], axis=-1, keepdims=True)
        s_own = jnp.where(colo == j, sj, s_own)
    s_own = s_own + (_t5_bias(qi - colo, rb_col) - far_bias)
    s_own = jnp.where(colo <= qi, s_own, NEG_INF)
    m = jnp.maximum(m, jnp.max(s_own, axis=-1, keepdims=True))

    p_own = jnp.exp(s_own - m)
    l = jnp.sum(p_own, axis=-1, keepdims=True)
    acc = jnp.zeros((rows, LANES), F32)
    for j in range(ls):
        acc = acc + p_own[:, j:j + 1] * v_new[j:j + 1, :]
    for c in range(n_chunks):
        p = jnp.exp(s_ref[:, c * ck:(c + 1) * ck] - m)
        l = l + jnp.sum(p, axis=-1, keepdims=True)
        acc = acc + _dot_nt(p.astype(BF16), vbuf[slot, :, c * ck:(c + 1) * ck].astype(BF16))
    o = acc / l
    oa, ob = _kv_lanes_to_head_cols([o[h * ls:(h + 1) * ls, :] for h in range(MOBA_HEADS)])
    o_ref[:, :LANES] = oa
    o_ref[:, LANES:] = ob
    drain()


def _mla_decode_kernel(pt_ref, q_ref, cn_ref, krn_ref, cc_hbm, rc_hbm, o_ref,
                       cbuf, rbuf, sem, s_ref, *, page_base, n_pages, page, ck, ls):
    slot, drain = _stream_pages(pt_ref, ((cc_hbm, cbuf, False), (rc_hbm, rbuf, True)), sem,
                                page_base, n_pages, page)
    nk = n_pages * page
    rows = MLA_HEADS * ls
    q = q_ref[...]
    qlat = jnp.concatenate([q[:, 2 * LANES * h:2 * LANES * h + LANES] for h in range(MLA_HEADS)], axis=0)
    qrp = jnp.concatenate([q[:, 2 * LANES * h + LANES:2 * LANES * (h + 1)] for h in range(MLA_HEADS)], axis=0)
    qr = qrp[:, :MLA_ROPE]
    qi = _iota((rows, 1), 0) % ls

    m = jnp.full((rows, 1), NEG_INF, F32)
    n_chunks = nk // ck
    for c in range(n_chunks):
        s = (_dot_nt(qlat, cbuf[slot, c * ck:(c + 1) * ck, :].astype(BF16))
             + _dot(qr, rbuf[slot, :, c * ck:(c + 1) * ck].astype(BF16)))
        s_ref[:, c * ck:(c + 1) * ck] = s
        m = jnp.maximum(m, jnp.max(s, axis=-1, keepdims=True))

    c_new = cn_ref[...]
    kr_new = krn_ref[...]
    colo = _iota((rows, ls), 1)
    qlat32 = qlat.astype(F32)
    qrp32 = qrp.astype(F32)
    s_own = jnp.zeros((rows, ls), F32)
    for j in range(ls):
        sj = (jnp.sum(qlat32 * c_new[j:j + 1, :], axis=-1, keepdims=True)
              + jnp.sum(qrp32 * kr_new[j:j + 1, :], axis=-1, keepdims=True))
        s_own = jnp.where(colo == j, sj, s_own)
    s_own = jnp.where(colo <= qi, s_own, NEG_INF)
    m = jnp.maximum(m, jnp.max(s_own, axis=-1, keepdims=True))

    p_own = jnp.exp(s_own - m)
    l = jnp.sum(p_own, axis=-1, keepdims=True)
    acc = jnp.zeros((rows, MLA_LAT), F32)
    for j in range(ls):
        acc = acc + p_own[:, j:j + 1] * c_new[j:j + 1, :]
    for c in range(n_chunks):
        p = jnp.exp(s_ref[:, c * ck:(c + 1) * ck] - m)
        l = l + jnp.sum(p, axis=-1, keepdims=True)
        acc = acc + _dot(p.astype(BF16), cbuf[slot, c * ck:(c + 1) * ck, :].astype(BF16))
    o = acc / l
    for h in range(MLA_HEADS):
        o_ref[:, MLA_LAT * h:MLA_LAT * (h + 1)] = o[h * ls:(h + 1) * ls, :]
    drain()


def _params(*sem):
    return pltpu.CompilerParams(dimension_semantics=sem, vmem_limit_bytes=VMEM_LIMIT_BYTES)


def _row_spec(tm, width):
    return pl.BlockSpec((tm, width), lambda i: (i, 0))


def _const_spec(shape):
    nd = len(shape)
    return pl.BlockSpec(shape, lambda i: (0,) * nd, pipeline_mode=pl.Buffered(1))


def _stacked_spec(shape, lead):
    nl, nd = len(lead), len(shape)
    return pl.BlockSpec((None,) * nl + tuple(shape[nl:]), lambda i: tuple(lead) + (0,) * (nd - nl),
                        pipeline_mode=pl.Buffered(1))


def _row_tile(t):
    tm = min(ROW_TILE, t)
    assert t % tm == 0 and tm % SUBLANES == 0, (t, tm)
    return tm


def _ffn_step(x, gpre, gpost, wg, wu, wd, lead):
    t, d = x.shape
    tm = _row_tile(t)
    return pl.pallas_call(
        _ffn_kernel, grid=(t // tm,),
        in_specs=[_row_spec(tm, d), _const_spec(gpre.shape), _const_spec(gpost.shape),
                  _stacked_spec(wg.shape, lead), _stacked_spec(wu.shape, lead), _stacked_spec(wd.shape, lead)],
        out_specs=_row_spec(tm, d), out_shape=jax.ShapeDtypeStruct((t, d), F32),
        compiler_params=_params("arbitrary"), name="ffn_step",
    )(x, gpre, gpost, wg, wu, wd)


def _ple_step(x, p, gpre, gpost, wpg, wpp, layer):
    t, d = x.shape
    tm = _row_tile(t)
    return pl.pallas_call(
        _ple_kernel, grid=(t // tm,),
        in_specs=[_row_spec(tm, d), pl.BlockSpec((None, tm, p.shape[2]), lambda i: (layer, i, 0)),
                  _const_spec(gpre.shape), _const_spec(gpost.shape),
                  _stacked_spec(wpg.shape, (layer,)), _stacked_spec(wpp.shape, (layer,))],
        out_specs=_row_spec(tm, d), out_shape=jax.ShapeDtypeStruct((t, d), F32),
        compiler_params=_params("arbitrary"), name="ple_step",
    )(x, p, gpre, gpost, wpg, wpp)


def _merge_step(x, o_moba, o_lat, o_conv, o_gmlp, gpre, gpost, wgate, wuv, wbr, wout, layer):
    t, d = x.shape
    tm = _row_tile(t)
    return pl.pallas_call(
        _merge_kernel, grid=(t // tm,),
        in_specs=[_row_spec(tm, d), _row_spec(tm, o_moba.shape[1]), _row_spec(tm, o_lat.shape[1]),
                  _row_spec(tm, o_conv.shape[1]), _row_spec(tm, o_gmlp.shape[1]),
                  _const_spec(gpre.shape), _const_spec(gpost.shape), _stacked_spec(wgate.shape, (layer,)),
                  _const_spec(wuv.shape), _stacked_spec(wbr.shape, (layer,)), _stacked_spec(wout.shape, (layer,))],
        out_specs=_row_spec(tm, d), out_shape=jax.ShapeDtypeStruct((t, d), F32),
        compiler_params=_params("arbitrary"), name="merge_step",
    )(x, o_moba, o_lat, o_conv, o_gmlp, gpre, gpost, wgate, wuv, wbr, wout)


def _mixer_pre(x, g, lw, cos_tab, sin_tab, st1, st2, wst, gbias, seq_len):
    t, d = x.shape
    tm = _row_tile(t)
    short_seq = seq_len < tm
    if short_seq:
        assert seq_len == SUBLANES and st1.shape == (t, 256), (seq_len, st1.shape)
        tiles_per_seq = 1
        st_spec = _row_spec(tm, 256)
        tab_spec = _const_spec(cos_tab.shape)
    else:
        assert seq_len % tm == 0
        tiles_per_seq = seq_len // tm
        st_spec = _const_spec(st1.shape)
        tab_spec = pl.BlockSpec((tm, LANES), lambda i: (i % tiles_per_seq, 0))
    ch = wst.shape[1]
    assert tm % ch == 0
    widths = [(256, F32), (128, F32), (128, F32), (256, BF16), (128, BF16), (128, F32), (128, F32),
              (1024, BF16), (256, BF16), (256, F32), (256, F32), (256, F32), (256, F32)]
    kern = functools.partial(_mixer_pre_kernel, tm=tm, short_seq=short_seq, tiles_per_seq=tiles_per_seq, ch=ch)
    return pl.pallas_call(
        kern, grid=(t // tm,),
        in_specs=[_row_spec(tm, d), _const_spec(g.shape), _const_spec(lw["w_in"].shape),
                  _const_spec(lw["qn"].shape), _const_spec(lw["kvn"].shape), _const_spec(lw["wq"].shape),
                  _const_spec(lw["wuk"].shape), tab_spec, tab_spec, _const_spec(lw["conv_w"].shape),
                  st_spec, st_spec, _const_spec(wst.shape), _const_spec(gbias.shape)],
        out_specs=[_row_spec(tm, w) for w, _ in widths],
        out_shape=[jax.ShapeDtypeStruct((t, w), dt) for w, dt in widths],
        scratch_shapes=[pltpu.VMEM((SUBLANES, 256), F32)],
        compiler_params=_params("arbitrary"), name="mixer_pre",
    )(x, g, lw["w_in"], lw["qn"], lw["kvn"], lw["wq"], lw["wuk"], cos_tab, sin_tab, lw["conv_w"],
      st1, st2, wst, gbias)


def _moba_prompt(rel_bias, q, k, kb, vb, batch, seq):
    bl = MOBA_BLOCK
    assert seq % bl == 0 and seq // bl <= LANES
    nb = seq // bl
    return pl.pallas_call(
        functools.partial(_moba_prompt_kernel, nb=nb), grid=(batch, nb),
        in_specs=[pl.BlockSpec(memory_space=pltpu.SMEM),
                  pl.BlockSpec((bl, 256), lambda b, i: (b * nb + i, 0)),
                  pl.BlockSpec((seq, 128), lambda b, i: (b, 0)),
                  pl.BlockSpec((seq, 256), lambda b, i: (b, 0)),
                  pl.BlockSpec((seq, 128), lambda b, i: (b, 0))],
        out_specs=pl.BlockSpec((bl, 256), lambda b, i: (b * nb + i, 0)),
        out_shape=jax.ShapeDtypeStruct((batch * seq, 256), F32),
        scratch_shapes=[pltpu.VMEM((LANES, LANES), F32),
                        pltpu.VMEM((MOBA_HEADS * bl, bl), F32), pltpu.VMEM((MOBA_HEADS * bl, bl), F32)],
        compiler_params=_params("arbitrary", "arbitrary"), name="moba_prompt",
    )(rel_bias, q, k, kb, vb)


def _mla_prompt(qcat, kcat, batch, seq):
    tq = min(ATT_TILE, seq)
    assert seq % tq == 0
    nq = seq // tq
    return pl.pallas_call(
        functools.partial(_mla_prompt_kernel, tq=tq), grid=(batch, nq),
        in_specs=[pl.BlockSpec((tq, qcat.shape[1]), lambda b, i: (b * nq + i, 0)),
                  pl.BlockSpec((seq, kcat.shape[1]), lambda b, i: (b, 0))],
        out_specs=pl.BlockSpec((tq, MLA_HEADS * MLA_LAT), lambda b, i: (b * nq + i, 0)),
        out_shape=jax.ShapeDtypeStruct((batch * seq, MLA_HEADS * MLA_LAT), F32),
        compiler_params=_params("arbitrary", "arbitrary"), name="mla_prompt",
    )(qcat, kcat)


def _decode_geometry(page_table, page, ls):
    bd, n_pages = page_table.shape
    nk = n_pages * page
    assert nk % MOBA_BLOCK == 0 and nk // MOBA_BLOCK <= LANES and ls == SUBLANES and page % LANES == 0, (nk, ls, page)
    ck = min(DEC_KEY_CHUNK, nk)
    assert nk % ck == 0
    return bd, n_pages, nk, ck


def _feature_major_pages(cache):
    depth, pool, page = cache.shape[:3]
    nd = cache.ndim
    return jnp.transpose(cache, (0, 1) + tuple(range(3, nd)) + (2,)).reshape(depth * pool, -1, page)


def _moba_decode(page_table, rel_bias, q, k_new, v_new, cache_k, cache_v, layer, ls):
    n_pool, page = cache_k.shape[1], cache_k.shape[2]
    bd, n_pages, nk, ck = _decode_geometry(page_table, page, ls)
    kc = _feature_major_pages(cache_k)
    vc = _feature_major_pages(cache_v)
    rows = MOBA_HEADS * ls
    kern = functools.partial(_moba_decode_kernel, page_base=layer * n_pool, n_pages=n_pages, page=page, ck=ck, ls=ls)
    grid_spec = pltpu.PrefetchScalarGridSpec(
        num_scalar_prefetch=1, grid=(bd,),
        in_specs=[pl.BlockSpec(memory_space=pltpu.SMEM),
                  pl.BlockSpec((ls, 256), lambda b, pt: (b, 0)),
                  pl.BlockSpec((ls, 128), lambda b, pt: (b, 0)),
                  pl.BlockSpec((ls, 128), lambda b, pt: (b, 0)),
                  pl.BlockSpec(memory_space=pl.ANY), pl.BlockSpec(memory_space=pl.ANY)],
        out_specs=pl.BlockSpec((ls, 256), lambda b, pt: (b, 0)),
        scratch_shapes=[pltpu.VMEM((2, 128, nk), F32), pltpu.VMEM((2, 128, nk), F32),
                        pltpu.SemaphoreType.DMA((2, 2)), pltpu.VMEM((rows, nk), F32)])
    return pl.pallas_call(
        kern, grid_spec=grid_spec, out_shape=jax.ShapeDtypeStruct((bd * ls, 256), F32),
        compiler_params=_params("arbitrary"), name="moba_decode",
    )(page_table, rel_bias, q, k_new, v_new, kc, vc)


def _mla_decode(page_table, qcat, c_new, kr_new, cache_lat, cache_rope, layer, ls):
    n_pool, page = cache_lat.shape[1], cache_lat.shape[2]
    bd, n_pages, nk, ck = _decode_geometry(page_table, page, ls)
    cc = cache_lat.reshape(-1, page, MLA_LAT)
    rc = _feature_major_pages(cache_rope)
    rows = MLA_HEADS * ls
    kern = functools.partial(_mla_decode_kernel, page_base=layer * n_pool, n_pages=n_pages, page=page, ck=ck, ls=ls)
    grid_spec = pltpu.PrefetchScalarGridSpec(
        num_scalar_prefetch=1, grid=(bd,),
        in_specs=[pl.BlockSpec((ls, qcat.shape[1]), lambda b, pt: (b, 0)),
                  pl.BlockSpec((ls, 128), lambda b, pt: (b, 0)),
                  pl.BlockSpec((ls, 128), lambda b, pt: (b, 0)),
                  pl.BlockSpec(memory_space=pl.ANY), pl.BlockSpec(memory_space=pl.ANY)],
        out_specs=pl.BlockSpec((ls, MLA_HEADS * MLA_LAT), lambda b, pt: (b, 0)),
        scratch_shapes=[pltpu.VMEM((2, nk, MLA_LAT), F32), pltpu.VMEM((2, MLA_ROPE, nk), F32),
                        pltpu.SemaphoreType.DMA((2, 2)), pltpu.VMEM((rows, nk), F32)])
    return pl.pallas_call(
        kern, grid_spec=grid_spec, out_shape=jax.ShapeDtypeStruct((bd * ls, MLA_HEADS * MLA_LAT), F32),
        compiler_params=_params("arbitrary"), name="mla_decode",
    )(page_table, qcat, c_new, kr_new, cc, rc)


def _prep_layer(i, norm_g, w_in, mla_q_norm, mla_kv_norm, w_mla_q_b, w_mla_kv_b, conv_w):
    d = w_in.shape[1]
    wi = w_in[i]
    mkr_lo, mkr_hi = 896, 928
    w_in_p = jnp.concatenate([wi[:, :mkr_lo], wi[:, mkr_hi:], wi[:, mkr_lo:mkr_hi],
                              jnp.zeros((d, LANES - MLA_ROPE), F32)], axis=1)
    assert w_in_p.shape[1] == C_END
    lat_q = w_mla_q_b.shape[1]
    wq4 = w_mla_q_b[i].reshape(lat_q, MLA_HEADS, MLA_NOPE + MLA_ROPE)
    wq_rope = jnp.pad(wq4[:, :, MLA_NOPE:], ((0, 0), (0, 0), (0, LANES - MLA_ROPE)))
    wq = jnp.concatenate([wq4[:, :, :MLA_NOPE].reshape(lat_q, -1), wq_rope.reshape(lat_q, -1)], axis=1)
    eye = jnp.eye(MLA_HEADS, dtype=F32)
    wkv = w_mla_kv_b[i]
    wuk = jnp.einsum("chn,hg->hngc", wkv[..., :MLA_NOPE], eye).reshape(MLA_HEADS * MLA_NOPE, MLA_HEADS * MLA_LAT)
    wuv = jnp.einsum("chv,hg->hcgv", wkv[..., MLA_NOPE:], eye).reshape(MLA_HEADS * MLA_LAT, -1)
    return {
        "g": norm_g[i][:, None, :], "w_in": w_in_p.astype(BF16),
        "qn": mla_q_norm[i][None, :], "kvn": mla_kv_norm[i][None, :],
        "wq": wq.astype(BF16), "wuk": wuk.astype(BF16), "wuv": wuv.astype(BF16), "conv_w": conv_w[i],
    }


def _rope_tables(pos):
    half = MLA_ROPE // 2
    freq = ROPE_THETA ** (-jnp.arange(half, dtype=F32) / half)
    ang = pos.astype(F32)[:, None] * freq
    cos, sin = jnp.cos(ang), jnp.sin(ang)
    pad = jnp.zeros((pos.shape[0], LANES - MLA_ROPE), F32)
    return jnp.concatenate([cos, cos, pad], axis=1), jnp.concatenate([-sin, sin, pad], axis=1)


def _gmlp_tables(ws, bs, seq_len, ch):
    n = min(seq_len, ch)
    w = (ws * jnp.tril(jnp.ones((GMLP_CHUNK, GMLP_CHUNK), ws.dtype)))[:, :n, :n]
    b = jnp.repeat(jnp.transpose(bs)[:n], 256 // GMLP_GROUPS, axis=1)
    reps = ch // n
    if reps > 1:
        w = jnp.einsum("gts,ab->gatbs", w, jnp.eye(reps, dtype=ws.dtype)).reshape(GMLP_GROUPS, ch, ch)
        b = jnp.tile(b, (reps, 1))
    return w.reshape(GMLP_GROUPS * ch, ch).astype(BF16), b


def _layer(i, x, p, lw, sw, tabs, attend):
    g = lw["g"]
    x = _ffn_step(x, g[0], g[1], sw["wfg"], sw["wfu"], sw["wfd"], (i, 0))
    (q, k, v, kb, vb, c_new, kr, qcat, kcat, o_conv, o_gmlp, vg, hc) = _mixer_pre(
        x, g[2], lw, tabs["cos"], tabs["sin"], tabs["st1"], tabs["st2"], tabs["wst"], tabs["gbias"], tabs["seq_len"])
    o_moba, o_lat = attend(q, k, v, kb, vb, c_new, kr, qcat, kcat)
    x = _merge_step(x, o_moba, o_lat, o_conv, o_gmlp, g[2], g[3], sw["w_gate"], lw["wuv"], sw["w_branch"],
                    sw["w_out"], i)
    x = _ffn_step(x, g[4], g[5], sw["wfg"], sw["wfu"], sw["wfd"], (i, 1))
    x = _ple_step(x, p, g[6], g[7], sw["wpg"], sw["wpp"], i)
    return x, (k, v, c_new, kr[:, :MLA_ROPE], hc, vg)


def kernel(x_prompt, x_sample, cache_moba_k, cache_moba_v, cache_mla_latent, cache_mla_rope, state_conv, page_table, p_prompt, p_sample, rel_bias, norm_g, w_ffn_gate, w_ffn_up, w_ffn_down, w_in, w_gate, mla_q_norm, mla_kv_norm, w_mla_q_b, w_mla_kv_b, conv_w, gmlp_ws, gmlp_b, w_branch, w_out, w_ple_gate, w_ple_proj):
    depth = norm_g.shape[0]
    bp, lp, d = x_prompt.shape
    bd, ls, _ = x_sample.shape
    page = cache_moba_k.shape[2]
    past_len = page_table.shape[1] * page
    tp, ts = bp * lp, bd * ls

    cos_p, sin_p = _rope_tables(jnp.arange(lp, dtype=jnp.int32))
    cos_s, sin_s = _rope_tables(past_len + jnp.arange(ls, dtype=jnp.int32))
    tm_s = _row_tile(ts)
    cos_s, sin_s = jnp.tile(cos_s, (tm_s // ls, 1)), jnp.tile(sin_s, (tm_s // ls, 1))
    ch_p = min(GMLP_CHUNK, _row_tile(tp))
    ch_s = min(GMLP_CHUNK, tm_s)
    no_state = jnp.zeros((SUBLANES, 256), F32)

    sw = {"wfg": w_ffn_gate.astype(BF16), "wfu": w_ffn_up.astype(BF16), "wfd": w_ffn_down.astype(BF16),
          "w_gate": w_gate.astype(BF16), "w_branch": w_branch.astype(BF16), "w_out": w_out.astype(BF16),
          "wpg": w_ple_gate.astype(BF16), "wpp": w_ple_proj.astype(BF16)}
    pp = p_prompt.reshape(depth, tp, -1)
    ps = p_sample.reshape(depth, ts, -1)

    yp = x_prompt.reshape(tp, d)
    ys = x_sample.reshape(ts, d)
    st_p, st_s = [], []
    for i in range(depth):
        lw = _prep_layer(i, norm_g, w_in, mla_q_norm, mla_kv_norm, w_mla_q_b, w_mla_kv_b, conv_w)
        wst_p, gb_p = _gmlp_tables(gmlp_ws[i], gmlp_b[i], lp, ch_p)
        wst_s, gb_s = _gmlp_tables(gmlp_ws[i], gmlp_b[i], ls, ch_s)
        prev = state_conv[i]
        st1 = jnp.pad(prev[:, 1:2], ((0, 0), (0, ls - 1), (0, 0))).reshape(ts, 256)
        st2 = jnp.pad(prev, ((0, 0), (0, ls - 2), (0, 0))).reshape(ts, 256)
        tabs_p = dict(cos=cos_p, sin=sin_p, st1=no_state, st2=no_state, wst=wst_p, gbias=gb_p, seq_len=lp)
        tabs_s = dict(cos=cos_s, sin=sin_s, st1=st1, st2=st2, wst=wst_s, gbias=gb_s, seq_len=ls)

        def attend_prompt(q, k, v, kb, vb, c_new, kr, qcat, kcat):
            return _moba_prompt(rel_bias, q, k, kb, vb, bp, lp), _mla_prompt(qcat, kcat, bp, lp)

        def attend_sample(q, k, v, kb, vb, c_new, kr, qcat, kcat, i=i):
            return (_moba_decode(page_table, rel_bias, q, k, v, cache_moba_k, cache_moba_v, i, ls),
                    _mla_decode(page_table, qcat, c_new, kr, cache_mla_latent, cache_mla_rope, i, ls))

        yp, sp = _layer(i, yp, pp, lw, sw, tabs_p, attend_prompt)
        ys, ss = _layer(i, ys, ps, lw, sw, tabs_s, attend_sample)
        st_p.append(sp)
        st_s.append(ss)

    def stack(states, j, shape):
        return jnp.stack([s[j].reshape(shape) for s in states], axis=0)

    kv_p, kv_s = (bp, lp, 2, MOBA_HD), (bd, ls, 2, MOBA_HD)
    conv_p = jnp.stack([s[4].reshape(bp, lp, 256)[:, -2:] for s in st_p], axis=0)
    conv_s = jnp.stack([s[4].reshape(bd, ls, 256)[:, -2:] for s in st_s], axis=0)
    return (yp.reshape(bp, lp, d), ys.reshape(bd, ls, d),
            stack(st_p, 0, kv_p), stack(st_p, 1, kv_p), stack(st_p, 2, (bp, lp, MLA_LAT)),
            stack(st_p, 3, (bp, lp, MLA_ROPE)), conv_p,
            stack(st_s, 0, kv_s), stack(st_s, 1, kv_s), stack(st_s, 2, (bd, ls, MLA_LAT)),
            stack(st_s, 3, (bd, ls, MLA_ROPE)), conv_s, stack(st_s, 5, (bd, ls, 256)))
```

```python
import functools
import math

import jax
import jax.numpy as jnp
from jax import lax
from jax.experimental import pallas as pl
from jax.experimental.pallas import tpu as pltpu

F32 = jnp.float32
BF16 = jnp.bfloat16
EPS = 1e-6
NEG_INF = float("-inf")
MASKED = -1e30

N_BRANCH = 4
MOBA_HEADS = 4
MOBA_HD = 64
MOBA_BLOCK = 256
MOBA_TOPK = 3
MLA_HEADS = 4
MLA_NOPE = 64
MLA_ROPE = 32
MLA_LAT = 128
ROPE_THETA = 10000.0
GMLP_GROUPS = 4
GMLP_CHUNK = 128
N_BUCKETS = 32
T5_MAX_DIST = 128
T5_EXACT = N_BUCKETS // 2

LANES = 128
SUBLANES = 8
VMEM_LIMIT_BYTES = 56 * 1024 * 1024

ROW_TILE = 512
ATT_TILE = 256
DEC_KEY_CHUNK = 4096

C_MQ, C_MK, C_MV, C_MQA, C_MKV, C_CB, C_CC, C_CX, C_GU, C_GV, C_MKR, C_END = (
    0, 256, 384, 512, 768, 896, 1152, 1408, 1664, 1920, 2176, 2304)

NT_DIMS = (((1,), (1,)), ((), ()))


def _rms(x, g):
    return x * lax.rsqrt(jnp.mean(x * x, axis=-1, keepdims=True) + EPS) * g


def _dot(a, b):
    return jnp.dot(a, b, preferred_element_type=F32)


def _dot_nt(a, b, precision=None):
    return lax.dot_general(a, b, NT_DIMS, precision=precision, preferred_element_type=F32)


def _iota(shape, dim):
    return lax.broadcasted_iota(jnp.int32, shape, dim)


def _t5_bias(dist, rb_of_bucket):
    n = jnp.maximum(dist, 0)
    nf = jnp.maximum(n, T5_EXACT).astype(F32)
    large = T5_EXACT + (jnp.log(nf / T5_EXACT) / math.log(T5_MAX_DIST / T5_EXACT)
                        * (N_BUCKETS - T5_EXACT)).astype(jnp.int32)
    bucket = jnp.where(n < T5_EXACT, n, jnp.minimum(large, N_BUCKETS - 1))
    out = jnp.zeros(dist.shape, F32)
    for k in range(N_BUCKETS):
        out = jnp.where(bucket == k, rb_of_bucket(k), out)
    return out


def _head_rows_to_kv_lanes(q):
    qa, qb = q[:, :LANES], q[:, LANES:]
    lo = _iota(qa.shape, 1) < MOBA_HD
    zero = jnp.zeros_like(qa)
    return [jnp.where(lo, qa, zero), jnp.where(lo, pltpu.roll(qa, MOBA_HD, 1), zero),
            jnp.where(lo, zero, pltpu.roll(qb, MOBA_HD, 1)), jnp.where(lo, zero, qb)]


def _kv_lanes_to_head_cols(o):
    lo = _iota(o[0].shape, 1) < MOBA_HD
    return (jnp.where(lo, o[0], pltpu.roll(o[1], MOBA_HD, 1)),
            jnp.where(lo, pltpu.roll(o[2], MOBA_HD, 1), o[3]))


def _top3_columns(work, col, ncol):
    picks = []
    colf = col.astype(F32)
    for _ in range(MOBA_TOPK):
        mx = jnp.max(work, axis=-1, keepdims=True)
        ok = mx > NEG_INF
        idx = jnp.min(jnp.where((work == mx) & ok, colf, float(ncol)), axis=-1, keepdims=True).astype(jnp.int32)
        work = jnp.where(col == idx, NEG_INF, work)
        picks.append((idx, ok))
    return picks


def _ffn_kernel(x_ref, gpre_ref, gpost_ref, wg_ref, wu_ref, wd_ref, o_ref):
    x = x_ref[...]
    xn = _rms(x, gpre_ref[...]).astype(BF16)
    g = _dot(xn, wg_ref[...])
    u = _dot(xn, wu_ref[...])
    a = (jax.nn.silu(g) * u).astype(BF16)
    y = _dot(a, wd_ref[...])
    o_ref[...] = x + 0.5 * _rms(y, gpost_ref[...])


def _ple_kernel(x_ref, p_ref, gpre_ref, gpost_ref, wpg_ref, wpp_ref, o_ref):
    x = x_ref[...]
    hp = _rms(x, gpre_ref[...]).astype(BF16)
    gate = jax.nn.sigmoid(_dot(hp, wpg_ref[...]))
    pp = _dot(p_ref[...].astype(BF16), wpp_ref[...])
    o_ref[...] = x + _rms(gate * pp, gpost_ref[...])


def _merge_kernel(x_ref, om_ref, ol_ref, oc_ref, og_ref, gpre_ref, gpost_ref,
                  wgate_ref, wuv_ref, wbr_ref, wout_ref, o_ref):
    x = x_ref[...]
    d = x.shape[1]
    h = _rms(x, gpre_ref[...]).astype(BF16)
    o_mla = _dot(ol_ref[...].astype(BF16), wuv_ref[...])
    branches = (om_ref[...], o_mla, oc_ref[...], og_ref[...])
    acc = None
    for n in range(N_BRANCH):
        gate = jax.nn.sigmoid(_dot(h, wgate_ref[:, n * d:(n + 1) * d]))
        term = gate * _dot(branches[n].astype(BF16), wbr_ref[n])
        acc = term if acc is None else acc + term
    out = _dot(acc.astype(BF16), wout_ref[...])
    o_ref[...] = x + _rms(out, gpost_ref[...])


def _mixer_pre_kernel(x_ref, g_ref, win_ref, qn_ref, kvn_ref, wq_ref, wuk_ref, cos_ref, sin_ref,
                      cw_ref, st1_ref, st2_ref, wst_ref, gb_ref,
                      q_o, k_o, v_o, kb_o, vb_o, c_o, kr_o, qcat_o, kcat_o, oc_o, og_o, vg_o, hc_o,
                      carry_ref, *, tm, short_seq, tiles_per_seq, ch):
    x = x_ref[...]
    h = _rms(x, g_ref[...]).astype(BF16)
    proj = _dot(h, win_ref[...])

    mk = proj[:, C_MK:C_MV]
    mv = proj[:, C_MV:C_MQA]
    q_o[...] = proj[:, C_MQ:C_MK]
    k_o[...] = mk
    v_o[...] = mv
    kb_o[:, :LANES] = mk.astype(BF16)
    if short_seq:
        kb_o[:, LANES:] = jnp.zeros((tm, LANES), BF16)
    else:
        pos = (pl.program_id(0) % tiles_per_seq) * tm + _iota((tm, LANES), 0)
        kb_o[:, LANES:] = jnp.where(pos // MOBA_BLOCK == _iota((tm, LANES), 1), 1.0, 0.0).astype(BF16)
    vb_o[...] = mv.astype(BF16)

    cos = cos_ref[...]
    sin = sin_ref[...]
    lane = _iota((tm, LANES), 1)
    half = MLA_ROPE // 2

    def rope(v):
        swapped = jnp.where(lane < half, pltpu.roll(v, LANES - half, 1), pltpu.roll(v, half, 1))
        return v * cos + swapped * sin

    cq = _rms(proj[:, C_MQA:C_MKV], qn_ref[...]).astype(BF16)
    qh = _dot(cq, wq_ref[...])
    nope_w = MLA_HEADS * MLA_NOPE
    qlat = _dot(qh[:, :nope_w].astype(BF16), wuk_ref[...])
    scale = (MLA_NOPE + MLA_ROPE) ** -0.5
    for hh in range(MLA_HEADS):
        qcat_o[:, 2 * LANES * hh:2 * LANES * hh + LANES] = (qlat[:, LANES * hh:LANES * (hh + 1)] * scale).astype(BF16)
        qr = rope(qh[:, nope_w + LANES * hh:nope_w + LANES * (hh + 1)])
        qcat_o[:, 2 * LANES * hh + LANES:2 * LANES * (hh + 1)] = (qr * scale).astype(BF16)
    c_new = _rms(proj[:, C_MKV:C_CB], kvn_ref[...])
    kr = rope(proj[:, C_MKR:C_END])
    c_o[...] = c_new
    kr_o[...] = kr
    kcat_o[:, :LANES] = c_new.astype(BF16)
    kcat_o[:, LANES:] = kr.astype(BF16)

    hc = proj[:, C_CC:C_CX] * proj[:, C_CX:C_GU]
    r1 = pltpu.roll(hc, 1, 0)
    r2 = pltpu.roll(hc, 2, 0)
    row = _iota(hc.shape, 0)
    if short_seq:
        rs = row % SUBLANES
        s1 = jnp.where(rs < 1, st1_ref[...], r1)
        s2 = jnp.where(rs < 2, st2_ref[...], r2)
    else:
        @pl.when(pl.program_id(0) % tiles_per_seq == 0)
        def _():
            carry_ref[...] = jnp.zeros(carry_ref.shape, F32)
        prev = carry_ref[...]
        s1 = jnp.where(row < 1, prev[SUBLANES - 1:SUBLANES, :], r1)
        s2 = jnp.where(row == 0, prev[SUBLANES - 2:SUBLANES - 1, :],
                       jnp.where(row == 1, prev[SUBLANES - 1:SUBLANES, :], r2))
        carry_ref[...] = hc[tm - SUBLANES:, :]
    cw = cw_ref[...]
    conv = cw[0:1, :] * s2 + cw[1:2, :] * s1 + cw[2:3, :] * hc
    oc_o[...] = proj[:, C_CB:C_CC] * conv
    hc_o[...] = hc

    u = jax.nn.gelu(proj[:, C_GU:C_GV])
    vg = jax.nn.gelu(proj[:, C_GV:C_MKR])
    vg_o[...] = vg
    wst = wst_ref[...]
    gb = gb_ref[...]
    group = _iota((ch, gb.shape[1]), 1) // (gb.shape[1] // GMLP_GROUPS)
    for c in range(tm // ch):
        mixed_all = _dot(wst, vg[c * ch:(c + 1) * ch, :].astype(BF16))
        mixed = gb
        for g in range(GMLP_GROUPS):
            mixed = mixed + jnp.where(group == g, mixed_all[g * ch:(g + 1) * ch, :], 0.0)
        og_o[c * ch:(c + 1) * ch, :] = u[c * ch:(c + 1) * ch, :] * mixed


def _attend_first(s, vt):
    m = jnp.max(s, axis=0, keepdims=True)
    p = jnp.exp(s - m)
    return m, jnp.sum(p, axis=0, keepdims=True), _dot(vt, p.astype(BF16))


def _attend_more(carry, s, vt):
    m, l, acc = carry
    m_new = jnp.maximum(m, jnp.max(s, axis=0, keepdims=True))
    alpha = jnp.exp(m - m_new)
    p = jnp.exp(s - m_new)
    return m_new, alpha * l + jnp.sum(p, axis=0, keepdims=True), alpha * acc + _dot(vt, p.astype(BF16))


def _moba_prompt_kernel(rb_ref, q_ref, k_ref, kb_ref, vt_ref, o_ref, km_ref, bown_ref, bprev_ref, *, nb):
    b = pl.program_id(0)
    i = pl.program_id(1)
    bl = MOBA_BLOCK
    rows = MOBA_HEADS * bl

    @pl.when((b == 0) & (i == 0))
    def _():
        dist = _iota((bl, bl), 1) - _iota((bl, bl), 0)
        for h in range(MOBA_HEADS):
            far = rb_ref[N_BUCKETS - 1, h]
            bown_ref[:, h * bl:(h + 1) * bl] = _t5_bias(dist, lambda k, h=h: rb_ref[k, h]) - far
            bprev_ref[:, h * bl:(h + 1) * bl] = _t5_bias(dist + bl, lambda k, h=h: rb_ref[k, h]) - far

    @pl.when(i == 0)
    def _():
        km_ref[...] = jnp.zeros(km_ref.shape, F32)
        for n in range(nb):
            km_ref[n:n + 1, :] = jnp.sum(k_ref[n * bl:(n + 1) * bl, :], axis=0, keepdims=True) * (1.0 / bl)

    qstack = jnp.concatenate(_head_rows_to_kv_lanes(q_ref[...]), axis=0)
    col = _iota((rows, LANES), 1)
    gate = _dot_nt(qstack, km_ref[...], precision=lax.Precision.HIGHEST)
    open_slot = col == i
    for idx, _ in _top3_columns(jnp.where(col < i, gate, NEG_INF), col, LANES):
        open_slot = open_slot | (col == idx)
    qs = jnp.concatenate([qstack * (MOBA_HD ** -0.5), jnp.where(open_slot, 0.0, MASKED)], axis=1).astype(BF16)

    def scores(n):
        return _dot_nt(kb_ref[pl.ds(pl.multiple_of(n * bl, bl), bl), :], qs)

    s = scores(i) + bown_ref[...]
    s = jnp.where(_iota((bl, rows), 0) <= _iota((bl, rows), 1) % bl, s, NEG_INF)
    carry = _attend_first(s, vt_ref[i])

    prev = jnp.maximum(i - 1, 0)
    prev_mask = jnp.where(i == 0, MASKED, 0.0)
    carry = _attend_more(carry, scores(prev) + (bprev_ref[...] + prev_mask), vt_ref[prev])

    m, l, acc = lax.fori_loop(0, prev, lambda n, c: _attend_more(c, scores(n), vt_ref[n]), carry)
    o = acc / l
    oa, ob = _kv_lanes_to_head_cols([o[:, h * bl:(h + 1) * bl].T for h in range(MOBA_HEADS)])
    o_ref[:, :LANES] = oa
    o_ref[:, LANES:] = ob


def _mla_prompt_kernel(q_ref, k_ref, vt_ref, o_ref, *, tq):
    i = pl.program_id(1)
    cols = MLA_HEADS * tq
    q = jnp.concatenate([q_ref[:, 2 * LANES * h:2 * LANES * (h + 1)] for h in range(MLA_HEADS)], axis=0)

    def scores(j):
        return _dot_nt(k_ref[pl.ds(pl.multiple_of(j * tq, tq), tq), :], q)

    s = jnp.where(_iota((tq, cols), 0) <= _iota((tq, cols), 1) % tq, scores(i), NEG_INF)
    carry = _attend_first(s, vt_ref[i])
    m, l, acc = lax.fori_loop(0, i, lambda j, c: _attend_more(c, scores(j), vt_ref[j]), carry)
    o = acc / l
    for h in range(MLA_HEADS):
        o_ref[:, MLA_LAT * h:MLA_LAT * (h + 1)] = o[:, h * tq:(h + 1) * tq].T


def _stream_pages(pt_ref, streams, sem_ref, page_base, n_pages, page):
    b = pl.program_id(0)
    last = pl.num_programs(0) - 1
    slot = b % 2
    nxt = jnp.minimum(b + 1, last)

    def copies(elem, sl, p):
        src = 0 if elem is None else page_base + pt_ref[elem, p]
        span = pl.ds(p * page, page)
        return [pltpu.make_async_copy(cache.at[src], buf.at[sl, :, span] if feature_major else buf.at[sl, span],
                                      sem_ref.at[sl, a])
                for a, (cache, buf, feature_major) in enumerate(streams)]

    def start_all(elem, sl):
        for p in range(n_pages):
            for a, cp in enumerate(copies(elem, sl, p)):
                cp.start(priority=a % 2)

    def wait_all(sl):
        for a in range(len(streams)):
            for p in range(n_pages):
                copies(None, sl, p)[a].wait()

    @pl.when(b == 0)
    def _():
        start_all(b, slot)

    wait_all(slot)
    start_all(nxt, 1 - slot)

    def drain():
        @pl.when(b == last)
        def _():
            wait_all(1 - slot)

    return slot, drain


def _moba_decode_kernel(pt_ref, rb_ref, q_ref, kn_ref, vn_ref, kc_hbm, vc_hbm, o_ref,
                        kbuf, vbuf, sem, s_ref, *, page_base, n_pages, page, ck, ls):
    slot, drain = _stream_pages(pt_ref, ((kc_hbm, kbuf, True), (vc_hbm, vbuf, True)), sem,
                                page_base, n_pages, page)
    bl = MOBA_BLOCK
    nk = n_pages * page
    nbc = nk // bl
    rows = MOBA_HEADS * ls

    qstack = jnp.concatenate(_head_rows_to_kv_lanes(q_ref[...]), axis=0)
    qs = (qstack * (MOBA_HD ** -0.5)).astype(BF16)

    slot_lane = _iota((LANES, LANES), 1)
    kmt = jnp.zeros((LANES, LANES), F32)
    for n in range(nbc):
        mean_n = jnp.sum(kbuf[slot, :, n * bl:(n + 1) * bl], axis=1, keepdims=True) * (1.0 / bl)
        kmt = jnp.where(slot_lane == n, mean_n, kmt)

    gate = jnp.dot(qstack, kmt, precision=lax.Precision.HIGHEST, preferred_element_type=F32)
    colg = _iota((rows, LANES), 1)
    picks = _top3_columns(jnp.where(colg < nbc, gate, NEG_INF), colg, LANES)

    rid = _iota((rows, 1), 0)
    head = rid // ls
    qi = rid % ls

    def rb_col(k):
        out = jnp.zeros((rows, 1), F32)
        for h in range(MOBA_HEADS):
            out = jnp.where(head == h, rb_ref[k, h], out)
        return out

    far_bias = rb_col(N_BUCKETS - 1)
    m = jnp.full((rows, 1), NEG_INF, F32)
    n_chunks = nk // ck
    blk_in_chunk = _iota((rows, ck), 1) // bl
    picked = [jnp.where(ok, idx, -1) for idx, ok in picks]
    for c in range(n_chunks):
        s = _dot(qs, kbuf[slot, :, c * ck:(c + 1) * ck].astype(BF16))
        first = c * (ck // bl)
        chosen = blk_in_chunk == picked[0] - first
        for idx in picked[1:]:
            chosen = chosen | (blk_in_chunk == idx - first)
        s = jnp.where(chosen, s, NEG_INF)
        s_ref[:, c * ck:(c + 1) * ck] = s
        m = jnp.maximum(m, jnp.max(s, axis=-1, keepdims=True))
    last = nk - bl
    dist_last = (bl + qi) - _iota((rows, bl), 1)
    s_last = s_ref[:, last:] + (_t5_bias(dist_last, rb_col) - far_bias)
    s_ref[:, last:] = s_last
    m = jnp.maximum(m, jnp.max(s_last, axis=-1, keepdims=True))

    k_new = kn_ref[...]
    v_new = vn_ref[...]
    colo = _iota((rows, ls), 1)
    q32 = qstack * (MOBA_HD ** -0.5)
    s_own = jnp.zeros((rows, ls), F32)
    for j in range(ls):
        sj = jnp.sum(q32 * k_new[j:j + 1, :], axis=-1, keepdims=True)
        s_own = jnp.where(colo == j, sj, s_own)
    s_own = s_own + (_t5_bias(qi - colo, rb_col) - far_bias)
    s_own = jnp.where(colo <= qi, s_own, NEG_INF)
    m = jnp.maximum(m, jnp.max(s_own, axis=-1, keepdims=True))

    p_own = jnp.exp(s_own - m)
    l = jnp.sum(p_own, axis=-1, keepdims=True)
    acc = jnp.zeros((rows, LANES), F32)
    for j in range(ls):
        acc = acc + p_own[:, j:j + 1] * v_new[j:j + 1, :]
    for c in range(n_chunks):
        p = jnp.exp(s_ref[:, c * ck:(c + 1) * ck] - m)
        l = l + jnp.sum(p, axis=-1, keepdims=True)
        acc = acc + _dot_nt(p.astype(BF16), vbuf[slot, :, c * ck:(c + 1) * ck].astype(BF16))
    o = acc / l
    oa, ob = _kv_lanes_to_head_cols([o[h * ls:(h + 1) * ls, :] for h in range(MOBA_HEADS)])
    o_ref[:, :LANES] = oa
    o_ref[:, LANES:] = ob
    drain()


def _mla_decode_kernel(pt_ref, q_ref, cn_ref, krn_ref, cc_hbm, rc_hbm, o_ref,
                       cbuf, rbuf, sem, s_ref, *, page_base, n_pages, page, ck, ls):
    slot, drain = _stream_pages(pt_ref, ((cc_hbm, cbuf, False), (rc_hbm, rbuf, True)), sem,
                                page_base, n_pages, page)
    nk = n_pages * page
    rows = MLA_HEADS * ls
    q = q_ref[...]
    qlat = jnp.concatenate([q[:, 2 * LANES * h:2 * LANES * h + LANES] for h in range(MLA_HEADS)], axis=0)
    qrp = jnp.concatenate([q[:, 2 * LANES * h + LANES:2 * LANES * (h + 1)] for h in range(MLA_HEADS)], axis=0)
    qr = qrp[:, :MLA_ROPE]
    qi = _iota((rows, 1), 0) % ls

    m = jnp.full((rows, 1), NEG_INF, F32)
    n_chunks = nk // ck
    for c in range(n_chunks):
        s = (_dot_nt(qlat, cbuf[slot, c * ck:(c + 1) * ck, :].astype(BF16))
             + _dot(qr, rbuf[slot, :, c * ck:(c + 1) * ck].astype(BF16)))
        s_ref[:, c * ck:(c + 1) * ck] = s
        m = jnp.maximum(m, jnp.max(s, axis=-1, keepdims=True))

    c_new = cn_ref[...]
    kr_new = krn_ref[...]
    colo = _iota((rows, ls), 1)
    qlat32 = qlat.astype(F32)
    qrp32 = qrp.astype(F32)
    s_own = jnp.zeros((rows, ls), F32)
    for j in range(ls):
        sj = (jnp.sum(qlat32 * c_new[j:j + 1, :], axis=-1, keepdims=True)
              + jnp.sum(qrp32 * kr_new[j:j + 1, :], axis=-1, keepdims=True))
        s_own = jnp.where(colo == j, sj, s_own)
    s_own = jnp.where(colo <= qi, s_own, NEG_INF)
    m = jnp.maximum(m, jnp.max(s_own, axis=-1, keepdims=True))

    p_own = jnp.exp(s_own - m)
    l = jnp.sum(p_own, axis=-1, keepdims=True)
    acc = jnp.zeros((rows, MLA_LAT), F32)
    for j in range(ls):
        acc = acc + p_own[:, j:j + 1] * c_new[j:j + 1, :]
    for c in range(n_chunks):
        p = jnp.exp(s_ref[:, c * ck:(c + 1) * ck] - m)
        l = l + jnp.sum(p, axis=-1, keepdims=True)
        acc = acc + _dot(p.astype(BF16), cbuf[slot, c * ck:(c + 1) * ck, :].astype(BF16))
    o = acc / l
    for h in range(MLA_HEADS):
        o_ref[:, MLA_LAT * h:MLA_LAT * (h + 1)] = o[h * ls:(h + 1) * ls, :]
    drain()


def _params(*sem):
    return pltpu.CompilerParams(dimension_semantics=sem, vmem_limit_bytes=VMEM_LIMIT_BYTES)


def _row_spec(tm, width):
    return pl.BlockSpec((tm, width), lambda i: (i, 0))


def _const_spec(shape):
    nd = len(shape)
    return pl.BlockSpec(shape, lambda i: (0,) * nd, pipeline_mode=pl.Buffered(1))


def _stacked_spec(shape, lead):
    nl, nd = len(lead), len(shape)
    return pl.BlockSpec((None,) * nl + tuple(shape[nl:]), lambda i: tuple(lead) + (0,) * (nd - nl),
                        pipeline_mode=pl.Buffered(1))


def _row_tile(t):
    tm = min(ROW_TILE, t)
    assert t % tm == 0 and tm % SUBLANES == 0, (t, tm)
    return tm


def _ffn_step(x, gpre, gpost, wg, wu, wd, lead):
    t, d = x.shape
    tm = _row_tile(t)
    return pl.pallas_call(
        _ffn_kernel, grid=(t // tm,),
        in_specs=[_row_spec(tm, d), _const_spec(gpre.shape), _const_spec(gpost.shape),
                  _stacked_spec(wg.shape, lead), _stacked_spec(wu.shape, lead), _stacked_spec(wd.shape, lead)],
        out_specs=_row_spec(tm, d), out_shape=jax.ShapeDtypeStruct((t, d), F32),
        compiler_params=_params("arbitrary"), name="ffn_step",
    )(x, gpre, gpost, wg, wu, wd)


def _ple_step(x, p, gpre, gpost, wpg, wpp, layer):
    t, d = x.shape
    tm = _row_tile(t)
    return pl.pallas_call(
        _ple_kernel, grid=(t // tm,),
        in_specs=[_row_spec(tm, d), pl.BlockSpec((None, tm, p.shape[2]), lambda i: (layer, i, 0)),
                  _const_spec(gpre.shape), _const_spec(gpost.shape),
                  _stacked_spec(wpg.shape, (layer,)), _stacked_spec(wpp.shape, (layer,))],
        out_specs=_row_spec(tm, d), out_shape=jax.ShapeDtypeStruct((t, d), F32),
        compiler_params=_params("arbitrary"), name="ple_step",
    )(x, p, gpre, gpost, wpg, wpp)


def _merge_step(x, o_moba, o_lat, o_conv, o_gmlp, gpre, gpost, wgate, wuv, wbr, wout, layer):
    t, d = x.shape
    tm = _row_tile(t)
    return pl.pallas_call(
        _merge_kernel, grid=(t // tm,),
        in_specs=[_row_spec(tm, d), _row_spec(tm, o_moba.shape[1]), _row_spec(tm, o_lat.shape[1]),
                  _row_spec(tm, o_conv.shape[1]), _row_spec(tm, o_gmlp.shape[1]),
                  _const_spec(gpre.shape), _const_spec(gpost.shape), _stacked_spec(wgate.shape, (layer,)),
                  _const_spec(wuv.shape), _stacked_spec(wbr.shape, (layer,)), _stacked_spec(wout.shape, (layer,))],
        out_specs=_row_spec(tm, d), out_shape=jax.ShapeDtypeStruct((t, d), F32),
        compiler_params=_params("arbitrary"), name="merge_step",
    )(x, o_moba, o_lat, o_conv, o_gmlp, gpre, gpost, wgate, wuv, wbr, wout)


def _mixer_pre(x, g, lw, cos_tab, sin_tab, st1, st2, wst, gbias, seq_len):
    t, d = x.shape
    tm = _row_tile(t)
    short_seq = seq_len < tm
    if short_seq:
        assert seq_len == SUBLANES and st1.shape == (t, 256), (seq_len, st1.shape)
        tiles_per_seq = 1
        st_spec = _row_spec(tm, 256)
        tab_spec = _const_spec(cos_tab.shape)
    else:
        assert seq_len % tm == 0
        tiles_per_seq = seq_len // tm
        st_spec = _const_spec(st1.shape)
        tab_spec = pl.BlockSpec((tm, LANES), lambda i: (i % tiles_per_seq, 0))
    ch = wst.shape[1]
    assert tm % ch == 0
    widths = [(256, F32), (128, F32), (128, F32), (256, BF16), (128, BF16), (128, F32), (128, F32),
              (1024, BF16), (256, BF16), (256, F32), (256, F32), (256, F32), (256, F32)]
    kern = functools.partial(_mixer_pre_kernel, tm=tm, short_seq=short_seq, tiles_per_seq=tiles_per_seq, ch=ch)
    return pl.pallas_call(
        kern, grid=(t // tm,),
        in_specs=[_row_spec(tm, d), _const_spec(g.shape), _const_spec(lw["w_in"].shape),
                  _const_spec(lw["qn"].shape), _const_spec(lw["kvn"].shape), _const_spec(lw["wq"].shape),
                  _const_spec(lw["wuk"].shape), tab_spec, tab_spec, _const_spec(lw["conv_w"].shape),
                  st_spec, st_spec, _const_spec(wst.shape), _const_spec(gbias.shape)],
        out_specs=[_row_spec(tm, w) for w, _ in widths],
        out_shape=[jax.ShapeDtypeStruct((t, w), dt) for w, dt in widths],
        scratch_shapes=[pltpu.VMEM((SUBLANES, 256), F32)],
        compiler_params=_params("arbitrary"), name="mixer_pre",
    )(x, g, lw["w_in"], lw["qn"], lw["kvn"], lw["wq"], lw["wuk"], cos_tab, sin_tab, lw["conv_w"],
      st1, st2, wst, gbias)


def _moba_prompt(rel_bias, q, k, kb, vb, batch, seq):
    bl = MOBA_BLOCK
    assert seq % bl == 0 and seq // bl <= LANES
    nb = seq // bl
    vt = jnp.swapaxes(vb.reshape(batch * nb, bl, vb.shape[1]), 1, 2)
    return pl.pallas_call(
        functools.partial(_moba_prompt_kernel, nb=nb), grid=(batch, nb),
        in_specs=[pl.BlockSpec(memory_space=pltpu.SMEM),
                  pl.BlockSpec((bl, 256), lambda b, i: (b * nb + i, 0)),
                  pl.BlockSpec((seq, 128), lambda b, i: (b, 0)),
                  pl.BlockSpec((seq, 256), lambda b, i: (b, 0)),
                  pl.BlockSpec((nb, vb.shape[1], bl), lambda b, i: (b, 0, 0))],
        out_specs=pl.BlockSpec((bl, 256), lambda b, i: (b * nb + i, 0)),
        out_shape=jax.ShapeDtypeStruct((batch * seq, 256), F32),
        scratch_shapes=[pltpu.VMEM((LANES, LANES), F32),
                        pltpu.VMEM((bl, MOBA_HEADS * bl), F32), pltpu.VMEM((bl, MOBA_HEADS * bl), F32)],
        compiler_params=_params("arbitrary", "arbitrary"), name="moba_prompt",
    )(rel_bias, q, k, kb, vt)


def _mla_prompt(qcat, kcat, batch, seq):
    tq = min(ATT_TILE, seq)
    assert seq % tq == 0
    nq = seq // tq
    vt = jnp.swapaxes(kcat[:, :MLA_LAT].reshape(batch * nq, tq, MLA_LAT), 1, 2)
    return pl.pallas_call(
        functools.partial(_mla_prompt_kernel, tq=tq), grid=(batch, nq),
        in_specs=[pl.BlockSpec((tq, qcat.shape[1]), lambda b, i: (b * nq + i, 0)),
                  pl.BlockSpec((seq, kcat.shape[1]), lambda b, i: (b, 0)),
                  pl.BlockSpec((nq, MLA_LAT, tq), lambda b, i: (b, 0, 0))],
        out_specs=pl.BlockSpec((tq, MLA_HEADS * MLA_LAT), lambda b, i: (b * nq + i, 0)),
        out_shape=jax.ShapeDtypeStruct((batch * seq, MLA_HEADS * MLA_LAT), F32),
        compiler_params=_params("arbitrary", "arbitrary"), name="mla_prompt",
    )(qcat, kcat, vt)


def _decode_geometry(page_table, page, ls):
    bd, n_pages = page_table.shape
    nk = n_pages * page
    assert nk % MOBA_BLOCK == 0 and nk // MOBA_BLOCK <= LANES and ls == SUBLANES and page % LANES == 0, (nk, ls, page)
    ck = min(DEC_KEY_CHUNK, nk)
    assert nk % ck == 0
    return bd, n_pages, nk, ck


def _feature_major_pages(cache):
    depth, pool, page = cache.shape[:3]
    nd = cache.ndim
    return jnp.transpose(cache, (0, 1) + tuple(range(3, nd)) + (2,)).reshape(depth * pool, -1, page)


def _moba_decode(page_table, rel_bias, q, k_new, v_new, cache_k, cache_v, layer, ls):
    n_pool, page = cache_k.shape[1], cache_k.shape[2]
    bd, n_pages, nk, ck = _decode_geometry(page_table, page, ls)
    kc = _feature_major_pages(cache_k)
    vc = _feature_major_pages(cache_v)
    rows = MOBA_HEADS * ls
    kern = functools.partial(_moba_decode_kernel, page_base=layer * n_pool, n_pages=n_pages, page=page, ck=ck, ls=ls)
    grid_spec = pltpu.PrefetchScalarGridSpec(
        num_scalar_prefetch=1, grid=(bd,),
        in_specs=[pl.BlockSpec(memory_space=pltpu.SMEM),
                  pl.BlockSpec((ls, 256), lambda b, pt: (b, 0)),
                  pl.BlockSpec((ls, 128), lambda b, pt: (b, 0)),
                  pl.BlockSpec((ls, 128), lambda b, pt: (b, 0)),
                  pl.BlockSpec(memory_space=pl.ANY), pl.BlockSpec(memory_space=pl.ANY)],
        out_specs=pl.BlockSpec((ls, 256), lambda b, pt: (b, 0)),
        scratch_shapes=[pltpu.VMEM((2, 128, nk), F32), pltpu.VMEM((2, 128, nk), F32),
                        pltpu.SemaphoreType.DMA((2, 2)), pltpu.VMEM((rows, nk), F32)])
    return pl.pallas_call(
        kern, grid_spec=grid_spec, out_shape=jax.ShapeDtypeStruct((bd * ls, 256), F32),
        compiler_params=_params("arbitrary"), name="moba_decode",
    )(page_table, rel_bias, q, k_new, v_new, kc, vc)


def _mla_decode(page_table, qcat, c_new, kr_new, cache_lat, cache_rope, layer, ls):
    n_pool, page = cache_lat.shape[1], cache_lat.shape[2]
    bd, n_pages, nk, ck = _decode_geometry(page_table, page, ls)
    cc = cache_lat.reshape(-1, page, MLA_LAT)
    rc = _feature_major_pages(cache_rope)
    rows = MLA_HEADS * ls
    kern = functools.partial(_mla_decode_kernel, page_base=layer * n_pool, n_pages=n_pages, page=page, ck=ck, ls=ls)
    grid_spec = pltpu.PrefetchScalarGridSpec(
        num_scalar_prefetch=1, grid=(bd,),
        in_specs=[pl.BlockSpec((ls, qcat.shape[1]), lambda b, pt: (b, 0)),
                  pl.BlockSpec((ls, 128), lambda b, pt: (b, 0)),
                  pl.BlockSpec((ls, 128), lambda b, pt: (b, 0)),
                  pl.BlockSpec(memory_space=pl.ANY), pl.BlockSpec(memory_space=pl.ANY)],
        out_specs=pl.BlockSpec((ls, MLA_HEADS * MLA_LAT), lambda b, pt: (b, 0)),
        scratch_shapes=[pltpu.VMEM((2, nk, MLA_LAT), F32), pltpu.VMEM((2, MLA_ROPE, nk), F32),
                        pltpu.SemaphoreType.DMA((2, 2)), pltpu.VMEM((rows, nk), F32)])
    return pl.pallas_call(
        kern, grid_spec=grid_spec, out_shape=jax.ShapeDtypeStruct((bd * ls, MLA_HEADS * MLA_LAT), F32),
        compiler_params=_params("arbitrary"), name="mla_decode",
    )(page_table, qcat, c_new, kr_new, cc, rc)


def _prep_layer(i, norm_g, w_in, mla_q_norm, mla_kv_norm, w_mla_q_b, w_mla_kv_b, conv_w):
    d = w_in.shape[1]
    wi = w_in[i]
    mkr_lo, mkr_hi = 896, 928
    w_in_p = jnp.concatenate([wi[:, :mkr_lo], wi[:, mkr_hi:], wi[:, mkr_lo:mkr_hi],
                              jnp.zeros((d, LANES - MLA_ROPE), F32)], axis=1)
    assert w_in_p.shape[1] == C_END
    lat_q = w_mla_q_b.shape[1]
    wq4 = w_mla_q_b[i].reshape(lat_q, MLA_HEADS, MLA_NOPE + MLA_ROPE)
    wq_rope = jnp.pad(wq4[:, :, MLA_NOPE:], ((0, 0), (0, 0), (0, LANES - MLA_ROPE)))
    wq = jnp.concatenate([wq4[:, :, :MLA_NOPE].reshape(lat_q, -1), wq_rope.reshape(lat_q, -1)], axis=1)
    eye = jnp.eye(MLA_HEADS, dtype=F32)
    wkv = w_mla_kv_b[i]
    wuk = jnp.einsum("chn,hg->hngc", wkv[..., :MLA_NOPE], eye).reshape(MLA_HEADS * MLA_NOPE, MLA_HEADS * MLA_LAT)
    wuv = jnp.einsum("chv,hg->hcgv", wkv[..., MLA_NOPE:], eye).reshape(MLA_HEADS * MLA_LAT, -1)
    return {
        "g": norm_g[i][:, None, :], "w_in": w_in_p.astype(BF16),
        "qn": mla_q_norm[i][None, :], "kvn": mla_kv_norm[i][None, :],
        "wq": wq.astype(BF16), "wuk": wuk.astype(BF16), "wuv": wuv.astype(BF16), "conv_w": conv_w[i],
    }


def _rope_tables(pos):
    half = MLA_ROPE // 2
    freq = ROPE_THETA ** (-jnp.arange(half, dtype=F32) / half)
    ang = pos.astype(F32)[:, None] * freq
    cos, sin = jnp.cos(ang), jnp.sin(ang)
    pad = jnp.zeros((pos.shape[0], LANES - MLA_ROPE), F32)
    return jnp.concatenate([cos, cos, pad], axis=1), jnp.concatenate([-sin, sin, pad], axis=1)


def _gmlp_tables(ws, bs, seq_len, ch):
    n = min(seq_len, ch)
    w = (ws * jnp.tril(jnp.ones((GMLP_CHUNK, GMLP_CHUNK), ws.dtype)))[:, :n, :n]
    b = jnp.repeat(jnp.transpose(bs)[:n], 256 // GMLP_GROUPS, axis=1)
    reps = ch // n
    if reps > 1:
        w = jnp.einsum("gts,ab->gatbs", w, jnp.eye(reps, dtype=ws.dtype)).reshape(GMLP_GROUPS, ch, ch)
        b = jnp.tile(b, (reps, 1))
    return w.reshape(GMLP_GROUPS * ch, ch).astype(BF16), b


def _layer(i, x, p, lw, sw, tabs, attend):
    g = lw["g"]
    x = _ffn_step(x, g[0], g[1], sw["wfg"], sw["wfu"], sw["wfd"], (i, 0))
    (q, k, v, kb, vb, c_new, kr, qcat, kcat, o_conv, o_gmlp, vg, hc) = _mixer_pre(
        x, g[2], lw, tabs["cos"], tabs["sin"], tabs["st1"], tabs["st2"], tabs["wst"], tabs["gbias"], tabs["seq_len"])
    o_moba, o_lat = attend(q, k, v, kb, vb, c_new, kr, qcat, kcat)
    x = _merge_step(x, o_moba, o_lat, o_conv, o_gmlp, g[2], g[3], sw["w_gate"], lw["wuv"], sw["w_branch"],
                    sw["w_out"], i)
    x = _ffn_step(x, g[4], g[5], sw["wfg"], sw["wfu"], sw["wfd"], (i, 1))
    x = _ple_step(x, p, g[6], g[7], sw["wpg"], sw["wpp"], i)
    return x, (k, v, c_new, kr[:, :MLA_ROPE], hc, vg)


def kernel(x_prompt, x_sample, cache_moba_k, cache_moba_v, cache_mla_latent, cache_mla_rope, state_conv, page_table, p_prompt, p_sample, rel_bias, norm_g, w_ffn_gate, w_ffn_up, w_ffn_down, w_in, w_gate, mla_q_norm, mla_kv_norm, w_mla_q_b, w_mla_kv_b, conv_w, gmlp_ws, gmlp_b, w_branch, w_out, w_ple_gate, w_ple_proj):
    depth = norm_g.shape[0]
    bp, lp, d = x_prompt.shape
    bd, ls, _ = x_sample.shape
    page = cache_moba_k.shape[2]
    past_len = page_table.shape[1] * page
    tp, ts = bp * lp, bd * ls

    cos_p, sin_p = _rope_tables(jnp.arange(lp, dtype=jnp.int32))
    cos_s, sin_s = _rope_tables(past_len + jnp.arange(ls, dtype=jnp.int32))
    tm_s = _row_tile(ts)
    cos_s, sin_s = jnp.tile(cos_s, (tm_s // ls, 1)), jnp.tile(sin_s, (tm_s // ls, 1))
    ch_p = min(GMLP_CHUNK, _row_tile(tp))
    ch_s = min(GMLP_CHUNK, tm_s)
    no_state = jnp.zeros((SUBLANES, 256), F32)

    sw = {"wfg": w_ffn_gate.astype(BF16), "wfu": w_ffn_up.astype(BF16), "wfd": w_ffn_down.astype(BF16),
          "w_gate": w_gate.astype(BF16), "w_branch": w_branch.astype(BF16), "w_out": w_out.astype(BF16),
          "wpg": w_ple_gate.astype(BF16), "wpp": w_ple_proj.astype(BF16)}
    pp = p_prompt.reshape(depth, tp, -1)
    ps = p_sample.reshape(depth, ts, -1)

    yp = x_prompt.reshape(tp, d)
    ys = x_sample.reshape(ts, d)
    st_p, st_s = [], []
    for i in range(depth):
        lw = _prep_layer(i, norm_g, w_in, mla_q_norm, mla_kv_norm, w_mla_q_b, w_mla_kv_b, conv_w)
        wst_p, gb_p = _gmlp_tables(gmlp_ws[i], gmlp_b[i], lp, ch_p)
        wst_s, gb_s = _gmlp_tables(gmlp_ws[i], gmlp_b[i], ls, ch_s)
        prev = state_conv[i]
        st1 = jnp.pad(prev[:, 1:2], ((0, 0), (0, ls - 1), (0, 0))).reshape(ts, 256)
        st2 = jnp.pad(prev, ((0, 0), (0, ls - 2), (0, 0))).reshape(ts, 256)
        tabs_p = dict(cos=cos_p, sin=sin_p, st1=no_state, st2=no_state, wst=wst_p, gbias=gb_p, seq_len=lp)
        tabs_s = dict(cos=cos_s, sin=sin_s, st1=st1, st2=st2, wst=wst_s, gbias=gb_s, seq_len=ls)

        def attend_prompt(q, k, v, kb, vb, c_new, kr, qcat, kcat):
            return _moba_prompt(rel_bias, q, k, kb, vb, bp, lp), _mla_prompt(qcat, kcat, bp, lp)

        def attend_sample(q, k, v, kb, vb, c_new, kr, qcat, kcat, i=i):
            return (_moba_decode(page_table, rel_bias, q, k, v, cache_moba_k, cache_moba_v, i, ls),
                    _mla_decode(page_table, qcat, c_new, kr, cache_mla_latent, cache_mla_rope, i, ls))

        yp, sp = _layer(i, yp, pp, lw, sw, tabs_p, attend_prompt)
        ys, ss = _layer(i, ys, ps, lw, sw, tabs_s, attend_sample)
        st_p.append(sp)
        st_s.append(ss)

    def stack(states, j, shape):
        return jnp.stack([s[j].reshape(shape) for s in states], axis=0)

    kv_p, kv_s = (bp, lp, 2, MOBA_HD), (bd, ls, 2, MOBA_HD)
    conv_p = jnp.stack([s[4].reshape(bp, lp, 256)[:, -2:] for s in st_p], axis=0)
    conv_s = jnp.stack([s[4].reshape(bd, ls, 256)[:, -2:] for s in st_s], axis=0)
    return (yp.reshape(bp, lp, d), ys.reshape(bd, ls, d),
            stack(st_p, 0, kv_p), stack(st_p, 1, kv_p), stack(st_p, 2, (bp, lp, MLA_LAT)),
            stack(st_p, 3, (bp, lp, MLA_ROPE)), conv_p,
            stack(st_s, 0, kv_s), stack(st_s, 1, kv_s), stack(st_s, 2, (bd, ls, MLA_LAT)),
            stack(st_s, 3, (bd, ls, MLA_ROPE)), conv_s, stack(st_s, 5, (bd, ls, 256)))
```

```python
import functools
import math

import jax
import jax.numpy as jnp
from jax import lax
from jax.experimental import pallas as pl
from jax.experimental.pallas import tpu as pltpu

F32 = jnp.float32
BF16 = jnp.bfloat16
EPS = 1e-6
NEG_INF = float("-inf")
MASKED = -1e30

N_BRANCH = 4
MOBA_HEADS = 4
MOBA_HD = 64
MOBA_BLOCK = 256
MOBA_TOPK = 3
MLA_HEADS = 4
MLA_NOPE = 64
MLA_ROPE = 32
MLA_LAT = 128
ROPE_THETA = 10000.0
GMLP_GROUPS = 4
GMLP_CHUNK = 128
N_BUCKETS = 32
T5_MAX_DIST = 128
T5_EXACT = N_BUCKETS // 2

LANES = 128
SUBLANES = 8
VMEM_LIMIT_BYTES = 56 * 1024 * 1024

ROW_TILE = 512
ATT_TILE = 256
DEC_KEY_CHUNK = 4096

C_MQ, C_MK, C_MV, C_MQA, C_MKV, C_CB, C_CC, C_CX, C_GU, C_GV, C_MKR, C_END = (
    0, 256, 384, 512, 768, 896, 1152, 1408, 1664, 1920, 2176, 2304)

NT_DIMS = (((1,), (1,)), ((), ()))


def _rms(x, g):
    return x * lax.rsqrt(jnp.mean(x * x, axis=-1, keepdims=True) + EPS) * g


def _dot(a, b):
    return jnp.dot(a, b, preferred_element_type=F32)


def _dot_nt(a, b, precision=None):
    return lax.dot_general(a, b, NT_DIMS, precision=precision, preferred_element_type=F32)


def _iota(shape, dim):
    return lax.broadcasted_iota(jnp.int32, shape, dim)


def _t5_bias(dist, rb_of_bucket):
    n = jnp.maximum(dist, 0)
    nf = jnp.maximum(n, T5_EXACT).astype(F32)
    large = T5_EXACT + (jnp.log(nf / T5_EXACT) / math.log(T5_MAX_DIST / T5_EXACT)
                        * (N_BUCKETS - T5_EXACT)).astype(jnp.int32)
    bucket = jnp.where(n < T5_EXACT, n, jnp.minimum(large, N_BUCKETS - 1))
    out = jnp.zeros(dist.shape, F32)
    for k in range(N_BUCKETS):
        out = jnp.where(bucket == k, rb_of_bucket(k), out)
    return out


def _head_rows_to_kv_lanes(q):
    qa, qb = q[:, :LANES], q[:, LANES:]
    lo = _iota(qa.shape, 1) < MOBA_HD
    zero = jnp.zeros_like(qa)
    return [jnp.where(lo, qa, zero), jnp.where(lo, pltpu.roll(qa, MOBA_HD, 1), zero),
            jnp.where(lo, zero, pltpu.roll(qb, MOBA_HD, 1)), jnp.where(lo, zero, qb)]


def _kv_lanes_to_head_cols(o):
    lo = _iota(o[0].shape, 1) < MOBA_HD
    return (jnp.where(lo, o[0], pltpu.roll(o[1], MOBA_HD, 1)),
            jnp.where(lo, pltpu.roll(o[2], MOBA_HD, 1), o[3]))


def _top3_columns(work, col, ncol):
    picks = []
    colf = col.astype(F32)
    for _ in range(MOBA_TOPK):
        mx = jnp.max(work, axis=-1, keepdims=True)
        ok = mx > NEG_INF
        idx = jnp.min(jnp.where((work == mx) & ok, colf, float(ncol)), axis=-1, keepdims=True).astype(jnp.int32)
        work = jnp.where(col == idx, NEG_INF, work)
        picks.append((idx, ok))
    return picks


def _ffn_half_step(x, gpre_ref, gpost_ref, wg_ref, wu_ref, wd_ref):
    xn = _rms(x, gpre_ref[...]).astype(BF16)
    g = _dot(xn, wg_ref[...])
    u = _dot(xn, wu_ref[...])
    a = (jax.nn.silu(g) * u).astype(BF16)
    y = _dot(a, wd_ref[...])
    return x + 0.5 * _rms(y, gpost_ref[...])


def _ffn_kernel(x_ref, gpre_ref, gpost_ref, wg_ref, wu_ref, wd_ref, o_ref):
    o_ref[...] = _ffn_half_step(x_ref[...], gpre_ref, gpost_ref, wg_ref, wu_ref, wd_ref)


def _ffn_ple_kernel(x_ref, gpre_ref, gpost_ref, wg_ref, wu_ref, wd_ref,
                    p_ref, gpre2_ref, gpost2_ref, wpg_ref, wpp_ref, o_ref):
    x = _ffn_half_step(x_ref[...], gpre_ref, gpost_ref, wg_ref, wu_ref, wd_ref)
    hp = _rms(x, gpre2_ref[...]).astype(BF16)
    gate = jax.nn.sigmoid(_dot(hp, wpg_ref[...]))
    pp = _dot(p_ref[...].astype(BF16), wpp_ref[...])
    o_ref[...] = x + _rms(gate * pp, gpost2_ref[...])


def _merge_kernel(x_ref, om_ref, ol_ref, oc_ref, og_ref, gpre_ref, gpost_ref,
                  wgate_ref, wuv_ref, wbr_ref, wout_ref, o_ref):
    x = x_ref[...]
    d = x.shape[1]
    h = _rms(x, gpre_ref[...]).astype(BF16)
    o_mla = _dot(ol_ref[...].astype(BF16), wuv_ref[...])
    branches = (om_ref[...], o_mla, oc_ref[...], og_ref[...])
    acc = None
    for n in range(N_BRANCH):
        gate = jax.nn.sigmoid(_dot(h, wgate_ref[:, n * d:(n + 1) * d]))
        term = gate * _dot(branches[n].astype(BF16), wbr_ref[n])
        acc = term if acc is None else acc + term
    out = _dot(acc.astype(BF16), wout_ref[...])
    o_ref[...] = x + _rms(out, gpost_ref[...])


def _mixer_pre_kernel(x_ref, g_ref, win_ref, qn_ref, kvn_ref, wq_ref, wuk_ref, cos_ref, sin_ref,
                      cw_ref, st1_ref, st2_ref, wst_ref, gb_ref,
                      q_o, k_o, v_o, kb_o, vb_o, c_o, kr_o, qcat_o, kcat_o, oc_o, og_o, vg_o, hc_o,
                      carry_ref, *, tm, short_seq, tiles_per_seq, ch):
    x = x_ref[...]
    h = _rms(x, g_ref[...]).astype(BF16)
    proj = _dot(h, win_ref[...])

    mk = proj[:, C_MK:C_MV]
    mv = proj[:, C_MV:C_MQA]
    q_o[...] = proj[:, C_MQ:C_MK]
    k_o[...] = mk
    v_o[...] = mv
    kb_o[:, :LANES] = mk.astype(BF16)
    if short_seq:
        kb_o[:, LANES:] = jnp.zeros((tm, LANES), BF16)
    else:
        pos = (pl.program_id(0) % tiles_per_seq) * tm + _iota((tm, LANES), 0)
        kb_o[:, LANES:] = jnp.where(pos // MOBA_BLOCK == _iota((tm, LANES), 1), 1.0, 0.0).astype(BF16)
    vb_o[...] = mv.astype(BF16)

    cos = cos_ref[...]
    sin = sin_ref[...]
    lane = _iota((tm, LANES), 1)
    half = MLA_ROPE // 2

    def rope(v):
        swapped = jnp.where(lane < half, pltpu.roll(v, LANES - half, 1), pltpu.roll(v, half, 1))
        return v * cos + swapped * sin

    cq = _rms(proj[:, C_MQA:C_MKV], qn_ref[...]).astype(BF16)
    qh = _dot(cq, wq_ref[...])
    nope_w = MLA_HEADS * MLA_NOPE
    qlat = _dot(qh[:, :nope_w].astype(BF16), wuk_ref[...])
    scale = (MLA_NOPE + MLA_ROPE) ** -0.5
    for hh in range(MLA_HEADS):
        qcat_o[:, 2 * LANES * hh:2 * LANES * hh + LANES] = (qlat[:, LANES * hh:LANES * (hh + 1)] * scale).astype(BF16)
        qr = rope(qh[:, nope_w + LANES * hh:nope_w + LANES * (hh + 1)])
        qcat_o[:, 2 * LANES * hh + LANES:2 * LANES * (hh + 1)] = (qr * scale).astype(BF16)
    c_new = _rms(proj[:, C_MKV:C_CB], kvn_ref[...])
    kr = rope(proj[:, C_MKR:C_END])
    c_o[...] = c_new
    kr_o[...] = kr
    kcat_o[:, :LANES] = c_new.astype(BF16)
    kcat_o[:, LANES:] = kr.astype(BF16)

    hc = proj[:, C_CC:C_CX] * proj[:, C_CX:C_GU]
    r1 = pltpu.roll(hc, 1, 0)
    r2 = pltpu.roll(hc, 2, 0)
    row = _iota(hc.shape, 0)
    if short_seq:
        rs = row % SUBLANES
        s1 = jnp.where(rs < 1, st1_ref[...], r1)
        s2 = jnp.where(rs < 2, st2_ref[...], r2)
    else:
        @pl.when(pl.program_id(0) % tiles_per_seq == 0)
        def _():
            carry_ref[...] = jnp.zeros(carry_ref.shape, F32)
        prev = carry_ref[...]
        s1 = jnp.where(row < 1, prev[SUBLANES - 1:SUBLANES, :], r1)
        s2 = jnp.where(row == 0, prev[SUBLANES - 2:SUBLANES - 1, :],
                       jnp.where(row == 1, prev[SUBLANES - 1:SUBLANES, :], r2))
        carry_ref[...] = hc[tm - SUBLANES:, :]
    cw = cw_ref[...]
    conv = cw[0:1, :] * s2 + cw[1:2, :] * s1 + cw[2:3, :] * hc
    oc_o[...] = proj[:, C_CB:C_CC] * conv
    hc_o[...] = hc

    u = jax.nn.gelu(proj[:, C_GU:C_GV])
    vg = jax.nn.gelu(proj[:, C_GV:C_MKR])
    vg_o[...] = vg
    wst = wst_ref[...]
    gb = gb_ref[...]
    group = _iota((ch, gb.shape[1]), 1) // (gb.shape[1] // GMLP_GROUPS)
    for c in range(tm // ch):
        mixed_all = _dot(wst, vg[c * ch:(c + 1) * ch, :].astype(BF16))
        mixed = gb
        for g in range(GMLP_GROUPS):
            mixed = mixed + jnp.where(group == g, mixed_all[g * ch:(g + 1) * ch, :], 0.0)
        og_o[c * ch:(c + 1) * ch, :] = u[c * ch:(c + 1) * ch, :] * mixed


def _attend_first(s, vt):
    m = jnp.max(s, axis=0, keepdims=True)
    p = jnp.exp(s - m)
    return m, jnp.sum(p, axis=0, keepdims=True), _dot(vt, p.astype(BF16))


def _attend_more(carry, s, vt):
    m, l, acc = carry
    m_new = jnp.maximum(m, jnp.max(s, axis=0, keepdims=True))
    alpha = jnp.exp(m - m_new)
    p = jnp.exp(s - m_new)
    return m_new, alpha * l + jnp.sum(p, axis=0, keepdims=True), alpha * acc + _dot(vt, p.astype(BF16))


def _moba_prompt_kernel(rb_ref, q_ref, k_ref, kb_ref, vt_ref, o_ref, km_ref, bown_ref, bprev_ref, *, nb):
    b = pl.program_id(0)
    i = pl.program_id(1)
    bl = MOBA_BLOCK
    rows = MOBA_HEADS * bl

    @pl.when((b == 0) & (i == 0))
    def _():
        dist = _iota((bl, bl), 1) - _iota((bl, bl), 0)
        for h in range(MOBA_HEADS):
            far = rb_ref[N_BUCKETS - 1, h]
            bown_ref[:, h * bl:(h + 1) * bl] = _t5_bias(dist, lambda k, h=h: rb_ref[k, h]) - far
            bprev_ref[:, h * bl:(h + 1) * bl] = _t5_bias(dist + bl, lambda k, h=h: rb_ref[k, h]) - far

    @pl.when(i == 0)
    def _():
        km_ref[...] = jnp.zeros(km_ref.shape, F32)
        for n in range(nb):
            km_ref[n:n + 1, :] = jnp.sum(k_ref[n * bl:(n + 1) * bl, :], axis=0, keepdims=True) * (1.0 / bl)

    qstack = jnp.concatenate(_head_rows_to_kv_lanes(q_ref[...]), axis=0)
    col = _iota((rows, LANES), 1)
    gate = _dot_nt(qstack, km_ref[...], precision=lax.Precision.HIGHEST)
    open_slot = col == i
    for idx, _ in _top3_columns(jnp.where(col < i, gate, NEG_INF), col, LANES):
        open_slot = open_slot | (col == idx)
    qs = jnp.concatenate([qstack * (MOBA_HD ** -0.5), jnp.where(open_slot, 0.0, MASKED)], axis=1).astype(BF16)

    def scores(n):
        return _dot_nt(kb_ref[pl.ds(pl.multiple_of(n * bl, bl), bl), :], qs)

    s = scores(i) + bown_ref[...]
    s = jnp.where(_iota((bl, rows), 0) <= _iota((bl, rows), 1) % bl, s, NEG_INF)
    carry = _attend_first(s, vt_ref[i])

    prev = jnp.maximum(i - 1, 0)
    prev_mask = jnp.where(i == 0, MASKED, 0.0)
    carry = _attend_more(carry, scores(prev) + (bprev_ref[...] + prev_mask), vt_ref[prev])

    m, l, acc = lax.fori_loop(0, prev, lambda n, c: _attend_more(c, scores(n), vt_ref[n]), carry)
    o = acc / l
    oa, ob = _kv_lanes_to_head_cols([o[:, h * bl:(h + 1) * bl].T for h in range(MOBA_HEADS)])
    o_ref[:, :LANES] = oa
    o_ref[:, LANES:] = ob


def _mla_prompt_kernel(q_ref, k_ref, vt_ref, o_ref, *, tq):
    i = pl.program_id(1)
    cols = MLA_HEADS * tq
    q = jnp.concatenate([q_ref[:, 2 * LANES * h:2 * LANES * (h + 1)] for h in range(MLA_HEADS)], axis=0)

    def scores(j):
        return _dot_nt(k_ref[pl.ds(pl.multiple_of(j * tq, tq), tq), :], q)

    s = jnp.where(_iota((tq, cols), 0) <= _iota((tq, cols), 1) % tq, scores(i), NEG_INF)
    carry = _attend_first(s, vt_ref[i])
    m, l, acc = lax.fori_loop(0, i, lambda j, c: _attend_more(c, scores(j), vt_ref[j]), carry)
    o = acc / l
    for h in range(MLA_HEADS):
        o_ref[:, MLA_LAT * h:MLA_LAT * (h + 1)] = o[:, h * tq:(h + 1) * tq].T


def _stream_pages(pt_ref, streams, sem_ref, page_base, n_pages, page):
    b = pl.program_id(0)
    last = pl.num_programs(0) - 1
    slot = b % 2
    nxt = jnp.minimum(b + 1, last)

    def copies(elem, sl, p):
        src = 0 if elem is None else page_base + pt_ref[elem, p]
        span = pl.ds(p * page, page)
        return [pltpu.make_async_copy(cache.at[src], buf.at[sl, :, span] if feature_major else buf.at[sl, span],
                                      sem_ref.at[sl, a])
                for a, (cache, buf, feature_major) in enumerate(streams)]

    def start_all(elem, sl):
        for p in range(n_pages):
            for a, cp in enumerate(copies(elem, sl, p)):
                cp.start(priority=a % 2)

    def wait_all(sl):
        for a in range(len(streams)):
            for p in range(n_pages):
                copies(None, sl, p)[a].wait()

    @pl.when(b == 0)
    def _():
        start_all(b, slot)

    wait_all(slot)
    start_all(nxt, 1 - slot)

    def drain():
        @pl.when(b == last)
        def _():
            wait_all(1 - slot)

    return slot, drain


def _moba_decode_kernel(pt_ref, rb_ref, q_ref, kn_ref, vn_ref, kc_hbm, vc_hbm, o_ref,
                        kbuf, vbuf, sem, s_ref, *, page_base, n_pages, page, ck, ls):
    slot, drain = _stream_pages(pt_ref, ((kc_hbm, kbuf, True), (vc_hbm, vbuf, True)), sem,
                                page_base, n_pages, page)
    bl = MOBA_BLOCK
    nk = n_pages * page
    nbc = nk // bl
    rows = MOBA_HEADS * ls

    qstack = jnp.concatenate(_head_rows_to_kv_lanes(q_ref[...]), axis=0)
    qs = (qstack * (MOBA_HD ** -0.5)).astype(BF16)

    slot_lane = _iota((LANES, LANES), 1)
    kmt = jnp.zeros((LANES, LANES), F32)
    for n in range(nbc):
        mean_n = jnp.sum(kbuf[slot, :, n * bl:(n + 1) * bl], axis=1, keepdims=True) * (1.0 / bl)
        kmt = jnp.where(slot_lane == n, mean_n, kmt)

    gate = jnp.dot(qstack, kmt, precision=lax.Precision.HIGHEST, preferred_element_type=F32)
    colg = _iota((rows, LANES), 1)
    picks = _top3_columns(jnp.where(colg < nbc, gate, NEG_INF), colg, LANES)

    rid = _iota((rows, 1), 0)
    head = rid // ls
    qi = rid % ls

    def rb_col(k):
        out = jnp.zeros((rows, 1), F32)
        for h in range(MOBA_HEADS):
            out = jnp.where(head == h, rb_ref[k, h], out)
        return out

    far_bias = rb_col(N_BUCKETS - 1)
    m = jnp.full((rows, 1), NEG_INF, F32)
    n_chunks = nk // ck
    blk_in_chunk = _iota((rows, ck), 1) // bl
    picked = [jnp.where(ok, idx, -1) for idx, ok in picks]
    for c in range(n_chunks):
        s = _dot(qs, kbuf[slot, :, c * ck:(c + 1) * ck].astype(BF16))
        first = c * (ck // bl)
        chosen = blk_in_chunk == picked[0] - first
        for idx in picked[1:]:
            chosen = chosen | (blk_in_chunk == idx - first)
        s = jnp.where(chosen, s, NEG_INF)
        s_ref[:, c * ck:(c + 1) * ck] = s
        m = jnp.maximum(m, jnp.max(s, axis=-1, keepdims=True))
    last = nk - bl
    dist_last = (bl + qi) - _iota((rows, bl), 1)
    s_last = s_ref[:, last:] + (_t5_bias(dist_last, rb_col) - far_bias)
    s_ref[:, last:] = s_last
    m = jnp.maximum(m, jnp.max(s_last, axis=-1, keepdims=True))

    k_new = kn_ref[...]
    v_new = vn_ref[...]
    colo = _iota((rows, ls), 1)
    q32 = qstack * (MOBA_HD ** -0.5)
    s_own = jnp.zeros((rows, ls), F32)
    for j in range(ls):
        sj = jnp.sum(q32 * k_new[j:j + 1, :], axis=-1, keepdims=True)
        s_own = jnp.where(colo == j, sj, s_own)
    s_own = s_own + (_t5_bias(qi - colo, rb_col) - far_bias)
    s_own = jnp.where(colo <= qi, s_own, NEG_INF)
    m = jnp.maximum(m, jnp.max(s_own, axis=-1, keepdims=True))

    p_own = jnp.exp(s_own - m)
    l = jnp.sum(p_own, axis=-1, keepdims=True)
    acc = jnp.zeros((rows, LANES), F32)
    for j in range(ls):
        acc = acc + p_own[:, j:j + 1] * v_new[j:j + 1, :]
    for c in range(n_chunks):
        p = jnp.exp(s_ref[:, c * ck:(c + 1) * ck] - m)
        l = l + jnp.sum(p, axis=-1, keepdims=True)
        acc = acc + _dot_nt(p.astype(BF16), vbuf[slot, :, c * ck:(c + 1) * ck].astype(BF16))
    o = acc / l
    oa, ob = _kv_lanes_to_head_cols([o[h * ls:(h + 1) * ls, :] for h in range(MOBA_HEADS)])
    o_ref[:, :LANES] = oa
    o_ref[:, LANES:] = ob
    drain()


def _mla_decode_kernel(pt_ref, q_ref, cn_ref, krn_ref, cc_hbm, rc_hbm, o_ref,
                       cbuf, rbuf, sem, s_ref, *, page_base, n_pages, page, ck, ls):
    slot, drain = _stream_pages(pt_ref, ((cc_hbm, cbuf, False), (rc_hbm, rbuf, True)), sem,
                                page_base, n_pages, page)
    nk = n_pages * page
    rows = MLA_HEADS * ls
    q = q_ref[...]
    qlat = jnp.concatenate([q[:, 2 * LANES * h:2 * LANES * h + LANES] for h in range(MLA_HEADS)], axis=0)
    qrp = jnp.concatenate([q[:, 2 * LANES * h + LANES:2 * LANES * (h + 1)] for h in range(MLA_HEADS)], axis=0)
    qr = qrp[:, :MLA_ROPE]
    qi = _iota((rows, 1), 0) % ls

    m = jnp.full((rows, 1), NEG_INF, F32)
    n_chunks = nk // ck
    for c in range(n_chunks):
        s = (_dot_nt(qlat, cbuf[slot, c * ck:(c + 1) * ck, :].astype(BF16))
             + _dot(qr, rbuf[slot, :, c * ck:(c + 1) * ck].astype(BF16)))
        s_ref[:, c * ck:(c + 1) * ck] = s
        m = jnp.maximum(m, jnp.max(s, axis=-1, keepdims=True))

    c_new = cn_ref[...]
    kr_new = krn_ref[...]
    colo = _iota((rows, ls), 1)
    qlat32 = qlat.astype(F32)
    qrp32 = qrp.astype(F32)
    s_own = jnp.zeros((rows, ls), F32)
    for j in range(ls):
        sj = (jnp.sum(qlat32 * c_new[j:j + 1, :], axis=-1, keepdims=True)
              + jnp.sum(qrp32 * kr_new[j:j + 1, :], axis=-1, keepdims=True))
        s_own = jnp.where(colo == j, sj, s_own)
    s_own = jnp.where(colo <= qi, s_own, NEG_INF)
    m = jnp.maximum(m, jnp.max(s_own, axis=-1, keepdims=True))

    p_own = jnp.exp(s_own - m)
    l = jnp.sum(p_own, axis=-1, keepdims=True)
    acc = jnp.zeros((rows, MLA_LAT), F32)
    for j in range(ls):
        acc = acc + p_own[:, j:j + 1] * c_new[j:j + 1, :]
    for c in range(n_chunks):
        p = jnp.exp(s_ref[:, c * ck:(c + 1) * ck] - m)
        l = l + jnp.sum(p, axis=-1, keepdims=True)
        acc = acc + _dot(p.astype(BF16), cbuf[slot, c * ck:(c + 1) * ck, :].astype(BF16))
    o = acc / l
    for h in range(MLA_HEADS):
        o_ref[:, MLA_LAT * h:MLA_LAT * (h + 1)] = o[h * ls:(h + 1) * ls, :]
    drain()


def _params(*sem):
    return pltpu.CompilerParams(dimension_semantics=sem, vmem_limit_bytes=VMEM_LIMIT_BYTES)


def _row_spec(tm, width):
    return pl.BlockSpec((tm, width), lambda i: (i, 0))


def _const_spec(shape):
    nd = len(shape)
    return pl.BlockSpec(shape, lambda i: (0,) * nd, pipeline_mode=pl.Buffered(1))


def _stacked_spec(shape, lead):
    nl, nd = len(lead), len(shape)
    return pl.BlockSpec((None,) * nl + tuple(shape[nl:]), lambda i: tuple(lead) + (0,) * (nd - nl),
                        pipeline_mode=pl.Buffered(1))


def _row_tile(t):
    tm = min(ROW_TILE, t)
    assert t % tm == 0 and tm % SUBLANES == 0, (t, tm)
    return tm


def _ffn_step(x, gpre, gpost, wg, wu, wd, lead):
    t, d = x.shape
    tm = _row_tile(t)
    return pl.pallas_call(
        _ffn_kernel, grid=(t // tm,),
        in_specs=[_row_spec(tm, d), _const_spec(gpre.shape), _const_spec(gpost.shape),
                  _stacked_spec(wg.shape, lead), _stacked_spec(wu.shape, lead), _stacked_spec(wd.shape, lead)],
        out_specs=_row_spec(tm, d), out_shape=jax.ShapeDtypeStruct((t, d), F32),
        compiler_params=_params("arbitrary"), name="ffn_step",
    )(x, gpre, gpost, wg, wu, wd)


def _ffn_ple_step(x, gpre, gpost, wg, wu, wd, lead, p, gpre2, gpost2, wpg, wpp):
    t, d = x.shape
    tm = _row_tile(t)
    layer = lead[0]
    return pl.pallas_call(
        _ffn_ple_kernel, grid=(t // tm,),
        in_specs=[_row_spec(tm, d), _const_spec(gpre.shape), _const_spec(gpost.shape),
                  _stacked_spec(wg.shape, lead), _stacked_spec(wu.shape, lead), _stacked_spec(wd.shape, lead),
                  pl.BlockSpec((None, tm, p.shape[2]), lambda i: (layer, i, 0)),
                  _const_spec(gpre2.shape), _const_spec(gpost2.shape),
                  _stacked_spec(wpg.shape, (layer,)), _stacked_spec(wpp.shape, (layer,))],
        out_specs=_row_spec(tm, d), out_shape=jax.ShapeDtypeStruct((t, d), F32),
        compiler_params=_params("arbitrary"), name="ffn_ple_step",
    )(x, gpre, gpost, wg, wu, wd, p, gpre2, gpost2, wpg, wpp)


def _merge_step(x, o_moba, o_lat, o_conv, o_gmlp, gpre, gpost, wgate, wuv, wbr, wout, layer):
    t, d = x.shape
    tm = _row_tile(t)
    return pl.pallas_call(
        _merge_kernel, grid=(t // tm,),
        in_specs=[_row_spec(tm, d), _row_spec(tm, o_moba.shape[1]), _row_spec(tm, o_lat.shape[1]),
                  _row_spec(tm, o_conv.shape[1]), _row_spec(tm, o_gmlp.shape[1]),
                  _const_spec(gpre.shape), _const_spec(gpost.shape), _stacked_spec(wgate.shape, (layer,)),
                  _const_spec(wuv.shape), _stacked_spec(wbr.shape, (layer,)), _stacked_spec(wout.shape, (layer,))],
        out_specs=_row_spec(tm, d), out_shape=jax.ShapeDtypeStruct((t, d), F32),
        compiler_params=_params("arbitrary"), name="merge_step",
    )(x, o_moba, o_lat, o_conv, o_gmlp, gpre, gpost, wgate, wuv, wbr, wout)


def _mixer_pre(x, g, lw, cos_tab, sin_tab, st1, st2, wst, gbias, seq_len):
    t, d = x.shape
    tm = _row_tile(t)
    short_seq = seq_len < tm
    if short_seq:
        assert seq_len == SUBLANES and st1.shape == (t, 256), (seq_len, st1.shape)
        tiles_per_seq = 1
        st_spec = _row_spec(tm, 256)
        tab_spec = _const_spec(cos_tab.shape)
    else:
        assert seq_len % tm == 0
        tiles_per_seq = seq_len // tm
        st_spec = _const_spec(st1.shape)
        tab_spec = pl.BlockSpec((tm, LANES), lambda i: (i % tiles_per_seq, 0))
    ch = wst.shape[1]
    assert tm % ch == 0
    widths = [(256, F32), (128, F32), (128, F32), (256, BF16), (128, BF16), (128, F32), (128, F32),
              (1024, BF16), (256, BF16), (256, F32), (256, F32), (256, F32), (256, F32)]
    kern = functools.partial(_mixer_pre_kernel, tm=tm, short_seq=short_seq, tiles_per_seq=tiles_per_seq, ch=ch)
    return pl.pallas_call(
        kern, grid=(t // tm,),
        in_specs=[_row_spec(tm, d), _const_spec(g.shape), _const_spec(lw["w_in"].shape),
                  _const_spec(lw["qn"].shape), _const_spec(lw["kvn"].shape), _const_spec(lw["wq"].shape),
                  _const_spec(lw["wuk"].shape), tab_spec, tab_spec, _const_spec(lw["conv_w"].shape),
                  st_spec, st_spec, _const_spec(wst.shape), _const_spec(gbias.shape)],
        out_specs=[_row_spec(tm, w) for w, _ in widths],
        out_shape=[jax.ShapeDtypeStruct((t, w), dt) for w, dt in widths],
        scratch_shapes=[pltpu.VMEM((SUBLANES, 256), F32)],
        compiler_params=_params("arbitrary"), name="mixer_pre",
    )(x, g, lw["w_in"], lw["qn"], lw["kvn"], lw["wq"], lw["wuk"], cos_tab, sin_tab, lw["conv_w"],
      st1, st2, wst, gbias)


def _moba_prompt(rel_bias, q, k, kb, vb, batch, seq):
    bl = MOBA_BLOCK
    assert seq % bl == 0 and seq // bl <= LANES
    nb = seq // bl
    vt = jnp.swapaxes(vb.reshape(batch * nb, bl, vb.shape[1]), 1, 2)
    return pl.pallas_call(
        functools.partial(_moba_prompt_kernel, nb=nb), grid=(batch, nb),
        in_specs=[pl.BlockSpec(memory_space=pltpu.SMEM),
                  pl.BlockSpec((bl, 256), lambda b, i: (b * nb + i, 0)),
                  pl.BlockSpec((seq, 128), lambda b, i: (b, 0)),
                  pl.BlockSpec((seq, 256), lambda b, i: (b, 0)),
                  pl.BlockSpec((nb, vb.shape[1], bl), lambda b, i: (b, 0, 0))],
        out_specs=pl.BlockSpec((bl, 256), lambda b, i: (b * nb + i, 0)),
        out_shape=jax.ShapeDtypeStruct((batch * seq, 256), F32),
        scratch_shapes=[pltpu.VMEM((LANES, LANES), F32),
                        pltpu.VMEM((bl, MOBA_HEADS * bl), F32), pltpu.VMEM((bl, MOBA_HEADS * bl), F32)],
        compiler_params=_params("arbitrary", "arbitrary"), name="moba_prompt",
    )(rel_bias, q, k, kb, vt)


def _mla_prompt(qcat, kcat, batch, seq):
    tq = min(ATT_TILE, seq)
    assert seq % tq == 0
    nq = seq // tq
    vt = jnp.swapaxes(kcat[:, :MLA_LAT].reshape(batch * nq, tq, MLA_LAT), 1, 2)
    return pl.pallas_call(
        functools.partial(_mla_prompt_kernel, tq=tq), grid=(batch, nq),
        in_specs=[pl.BlockSpec((tq, qcat.shape[1]), lambda b, i: (b * nq + i, 0)),
                  pl.BlockSpec((seq, kcat.shape[1]), lambda b, i: (b, 0)),
                  pl.BlockSpec((nq, MLA_LAT, tq), lambda b, i: (b, 0, 0))],
        out_specs=pl.BlockSpec((tq, MLA_HEADS * MLA_LAT), lambda b, i: (b * nq + i, 0)),
        out_shape=jax.ShapeDtypeStruct((batch * seq, MLA_HEADS * MLA_LAT), F32),
        compiler_params=_params("arbitrary", "arbitrary"), name="mla_prompt",
    )(qcat, kcat, vt)


def _decode_geometry(page_table, page, ls):
    bd, n_pages = page_table.shape
    nk = n_pages * page
    assert nk % MOBA_BLOCK == 0 and nk // MOBA_BLOCK <= LANES and ls == SUBLANES and page % LANES == 0, (nk, ls, page)
    ck = min(DEC_KEY_CHUNK, nk)
    assert nk % ck == 0
    return bd, n_pages, nk, ck


def _feature_major_pages(cache):
    depth, pool, page = cache.shape[:3]
    nd = cache.ndim
    return jnp.transpose(cache, (0, 1) + tuple(range(3, nd)) + (2,)).reshape(depth * pool, -1, page)


def _moba_decode(page_table, rel_bias, q, k_new, v_new, cache_k, cache_v, layer, ls):
    n_pool, page = cache_k.shape[1], cache_k.shape[2]
    bd, n_pages, nk, ck = _decode_geometry(page_table, page, ls)
    kc = _feature_major_pages(cache_k)
    vc = _feature_major_pages(cache_v)
    rows = MOBA_HEADS * ls
    kern = functools.partial(_moba_decode_kernel, page_base=layer * n_pool, n_pages=n_pages, page=page, ck=ck, ls=ls)
    grid_spec = pltpu.PrefetchScalarGridSpec(
        num_scalar_prefetch=1, grid=(bd,),
        in_specs=[pl.BlockSpec(memory_space=pltpu.SMEM),
                  pl.BlockSpec((ls, 256), lambda b, pt: (b, 0)),
                  pl.BlockSpec((ls, 128), lambda b, pt: (b, 0)),
                  pl.BlockSpec((ls, 128), lambda b, pt: (b, 0)),
                  pl.BlockSpec(memory_space=pl.ANY), pl.BlockSpec(memory_space=pl.ANY)],
        out_specs=pl.BlockSpec((ls, 256), lambda b, pt: (b, 0)),
        scratch_shapes=[pltpu.VMEM((2, 128, nk), F32), pltpu.VMEM((2, 128, nk), F32),
                        pltpu.SemaphoreType.DMA((2, 2)), pltpu.VMEM((rows, nk), F32)])
    return pl.pallas_call(
        kern, grid_spec=grid_spec, out_shape=jax.ShapeDtypeStruct((bd * ls, 256), F32),
        compiler_params=_params("arbitrary"), name="moba_decode",
    )(page_table, rel_bias, q, k_new, v_new, kc, vc)


def _mla_decode(page_table, qcat, c_new, kr_new, cache_lat, cache_rope, layer, ls):
    n_pool, page = cache_lat.shape[1], cache_lat.shape[2]
    bd, n_pages, nk, ck = _decode_geometry(page_table, page, ls)
    cc = cache_lat.reshape(-1, page, MLA_LAT)
    rc = _feature_major_pages(cache_rope)
    rows = MLA_HEADS * ls
    kern = functools.partial(_mla_decode_kernel, page_base=layer * n_pool, n_pages=n_pages, page=page, ck=ck, ls=ls)
    grid_spec = pltpu.PrefetchScalarGridSpec(
        num_scalar_prefetch=1, grid=(bd,),
        in_specs=[pl.BlockSpec((ls, qcat.shape[1]), lambda b, pt: (b, 0)),
                  pl.BlockSpec((ls, 128), lambda b, pt: (b, 0)),
                  pl.BlockSpec((ls, 128), lambda b, pt: (b, 0)),
                  pl.BlockSpec(memory_space=pl.ANY), pl.BlockSpec(memory_space=pl.ANY)],
        out_specs=pl.BlockSpec((ls, MLA_HEADS * MLA_LAT), lambda b, pt: (b, 0)),
        scratch_shapes=[pltpu.VMEM((2, nk, MLA_LAT), F32), pltpu.VMEM((2, MLA_ROPE, nk), F32),
                        pltpu.SemaphoreType.DMA((2, 2)), pltpu.VMEM((rows, nk), F32)])
    return pl.pallas_call(
        kern, grid_spec=grid_spec, out_shape=jax.ShapeDtypeStruct((bd * ls, MLA_HEADS * MLA_LAT), F32),
        compiler_params=_params("arbitrary"), name="mla_decode",
    )(page_table, qcat, c_new, kr_new, cc, rc)


def _prep_layer(i, norm_g, w_in, mla_q_norm, mla_kv_norm, w_mla_q_b, w_mla_kv_b, conv_w):
    d = w_in.shape[1]
    wi = w_in[i]
    mkr_lo, mkr_hi = 896, 928
    w_in_p = jnp.concatenate([wi[:, :mkr_lo], wi[:, mkr_hi:], wi[:, mkr_lo:mkr_hi],
                              jnp.zeros((d, LANES - MLA_ROPE), F32)], axis=1)
    assert w_in_p.shape[1] == C_END
    lat_q = w_mla_q_b.shape[1]
    wq4 = w_mla_q_b[i].reshape(lat_q, MLA_HEADS, MLA_NOPE + MLA_ROPE)
    wq_rope = jnp.pad(wq4[:, :, MLA_NOPE:], ((0, 0), (0, 0), (0, LANES - MLA_ROPE)))
    wq = jnp.concatenate([wq4[:, :, :MLA_NOPE].reshape(lat_q, -1), wq_rope.reshape(lat_q, -1)], axis=1)
    eye = jnp.eye(MLA_HEADS, dtype=F32)
    wkv = w_mla_kv_b[i]
    wuk = jnp.einsum("chn,hg->hngc", wkv[..., :MLA_NOPE], eye).reshape(MLA_HEADS * MLA_NOPE, MLA_HEADS * MLA_LAT)
    wuv = jnp.einsum("chv,hg->hcgv", wkv[..., MLA_NOPE:], eye).reshape(MLA_HEADS * MLA_LAT, -1)
    return {
        "g": norm_g[i][:, None, :], "w_in": w_in_p.astype(BF16),
        "qn": mla_q_norm[i][None, :], "kvn": mla_kv_norm[i][None, :],
        "wq": wq.astype(BF16), "wuk": wuk.astype(BF16), "wuv": wuv.astype(BF16), "conv_w": conv_w[i],
    }


def _rope_tables(pos):
    half = MLA_ROPE // 2
    freq = ROPE_THETA ** (-jnp.arange(half, dtype=F32) / half)
    ang = pos.astype(F32)[:, None] * freq
    cos, sin = jnp.cos(ang), jnp.sin(ang)
    pad = jnp.zeros((pos.shape[0], LANES - MLA_ROPE), F32)
    return jnp.concatenate([cos, cos, pad], axis=1), jnp.concatenate([-sin, sin, pad], axis=1)


def _gmlp_tables(ws, bs, seq_len, ch):
    n = min(seq_len, ch)
    w = (ws * jnp.tril(jnp.ones((GMLP_CHUNK, GMLP_CHUNK), ws.dtype)))[:, :n, :n]
    b = jnp.repeat(jnp.transpose(bs)[:n], 256 // GMLP_GROUPS, axis=1)
    reps = ch // n
    if reps > 1:
        w = jnp.einsum("gts,ab->gatbs", w, jnp.eye(reps, dtype=ws.dtype)).reshape(GMLP_GROUPS, ch, ch)
        b = jnp.tile(b, (reps, 1))
    return w.reshape(GMLP_GROUPS * ch, ch).astype(BF16), b


def _layer(i, x, p, lw, sw, tabs, attend):
    g = lw["g"]
    x = _ffn_step(x, g[0], g[1], sw["wfg"], sw["wfu"], sw["wfd"], (i, 0))
    (q, k, v, kb, vb, c_new, kr, qcat, kcat, o_conv, o_gmlp, vg, hc) = _mixer_pre(
        x, g[2], lw, tabs["cos"], tabs["sin"], tabs["st1"], tabs["st2"], tabs["wst"], tabs["gbias"], tabs["seq_len"])
    o_moba, o_lat = attend(q, k, v, kb, vb, c_new, kr, qcat, kcat)
    x = _merge_step(x, o_moba, o_lat, o_conv, o_gmlp, g[2], g[3], sw["w_gate"], lw["wuv"], sw["w_branch"],
                    sw["w_out"], i)
    x = _ffn_ple_step(x, g[4], g[5], sw["wfg"], sw["wfu"], sw["wfd"], (i, 1), p, g[6], g[7], sw["wpg"], sw["wpp"])
    return x, (k, v, c_new, kr[:, :MLA_ROPE], hc, vg)


def kernel(x_prompt, x_sample, cache_moba_k, cache_moba_v, cache_mla_latent, cache_mla_rope, state_conv, page_table, p_prompt, p_sample, rel_bias, norm_g, w_ffn_gate, w_ffn_up, w_ffn_down, w_in, w_gate, mla_q_norm, mla_kv_norm, w_mla_q_b, w_mla_kv_b, conv_w, gmlp_ws, gmlp_b, w_branch, w_out, w_ple_gate, w_ple_proj):
    depth = norm_g.shape[0]
    bp, lp, d = x_prompt.shape
    bd, ls, _ = x_sample.shape
    page = cache_moba_k.shape[2]
    past_len = page_table.shape[1] * page
    tp, ts = bp * lp, bd * ls

    cos_p, sin_p = _rope_tables(jnp.arange(lp, dtype=jnp.int32))
    cos_s, sin_s = _rope_tables(past_len + jnp.arange(ls, dtype=jnp.int32))
    tm_s = _row_tile(ts)
    cos_s, sin_s = jnp.tile(cos_s, (tm_s // ls, 1)), jnp.tile(sin_s, (tm_s // ls, 1))
    ch_p = min(GMLP_CHUNK, _row_tile(tp))
    ch_s = min(GMLP_CHUNK, tm_s)
    no_state = jnp.zeros((SUBLANES, 256), F32)

    sw = {"wfg": w_ffn_gate.astype(BF16), "wfu": w_ffn_up.astype(BF16), "wfd": w_ffn_down.astype(BF16),
          "w_gate": w_gate.astype(BF16), "w_branch": w_branch.astype(BF16), "w_out": w_out.astype(BF16),
          "wpg": w_ple_gate.astype(BF16), "wpp": w_ple_proj.astype(BF16)}
    pp = p_prompt.reshape(depth, tp, -1)
    ps = p_sample.reshape(depth, ts, -1)

    yp = x_prompt.reshape(tp, d)
    ys = x_sample.reshape(ts, d)
    st_p, st_s = [], []
    for i in range(depth):
        lw = _prep_layer(i, norm_g, w_in, mla_q_norm, mla_kv_norm, w_mla_q_b, w_mla_kv_b, conv_w)
        wst_p, gb_p = _gmlp_tables(gmlp_ws[i], gmlp_b[i], lp, ch_p)
        wst_s, gb_s = _gmlp_tables(gmlp_ws[i], gmlp_b[i], ls, ch_s)
        prev = state_conv[i]
        st1 = jnp.pad(prev[:, 1:2], ((0, 0), (0, ls - 1), (0, 0))).reshape(ts, 256)
        st2 = jnp.pad(prev, ((0, 0), (0, ls - 2), (0, 0))).reshape(ts, 256)
        tabs_p = dict(cos=cos_p, sin=sin_p, st1=no_state, st2=no_state, wst=wst_p, gbias=gb_p, seq_len=lp)
        tabs_s = dict(cos=cos_s, sin=sin_s, st1=st1, st2=st2, wst=wst_s, gbias=gb_s, seq_len=ls)

        def attend_prompt(q, k, v, kb, vb, c_new, kr, qcat, kcat):
            return _moba_prompt(rel_bias, q, k, kb, vb, bp, lp), _mla_prompt(qcat, kcat, bp, lp)

        def attend_sample(q, k, v, kb, vb, c_new, kr, qcat, kcat, i=i):
            return (_moba_decode(page_table, rel_bias, q, k, v, cache_moba_k, cache_moba_v, i, ls),
                    _mla_decode(page_table, qcat, c_new, kr, cache_mla_latent, cache_mla_rope, i, ls))

        yp, sp = _layer(i, yp, pp, lw, sw, tabs_p, attend_prompt)
        ys, ss = _layer(i, ys, ps, lw, sw, tabs_s, attend_sample)
        st_p.append(sp)
        st_s.append(ss)

    def stack(states, j, shape):
        return jnp.stack([s[j].reshape(shape) for s in states], axis=0)

    kv_p, kv_s = (bp, lp, 2, MOBA_HD), (bd, ls, 2, MOBA_HD)
    conv_p = jnp.stack([s[4].reshape(bp, lp, 256)[:, -2:] for s in st_p], axis=0)
    conv_s = jnp.stack([s[4].reshape(bd, ls, 256)[:, -2:] for s in st_s], axis=0)
    return (yp.reshape(bp, lp, d), ys.reshape(bd, ls, d),
            stack(st_p, 0, kv_p), stack(st_p, 1, kv_p), stack(st_p, 2, (bp, lp, MLA_LAT)),
            stack(st_p, 3, (bp, lp, MLA_ROPE)), conv_p,
            stack(st_s, 0, kv_s), stack(st_s, 1, kv_s), stack(st_s, 2, (bd, ls, MLA_LAT)),
            stack(st_s, 3, (bd, ls, MLA_ROPE)), conv_s, stack(st_s, 5, (bd, ls, 256)))
```
